```python
import math
import jax, jax.numpy as jnp
from jax import lax
import numpy as np

D_MODEL = 4096
BATCH = 8
SEQ = 2048
DEPTH = 4
DEC_BATCH = 8
DEC_SEQ = 16
PAST_LEN = 2048

CHUNK = 64
Q_BLOCK = 128
D_MIX = D_MODEL
D_GLA = D_MIX // 2
H_G = 4
DK_G = D_GLA // 2 // H_G
DV_G = D_GLA // H_G
GATE_RANK = 16
GATE_TAU = 16.0
D_DIFF = D_MIX - D_GLA
H_D = 8
HD_D = D_DIFF // H_D // 2
DV_D = 2 * HD_D
ROT_DIM = HD_D // 4
ROPE_THETA = 500000.0
D_FF = 4 * D_MODEL
EPS = 1e-5
W_GQ = H_G * DK_G
W_GK = H_G * DK_G
W_GV = D_GLA
W_GG = D_GLA
W_GA = GATE_RANK
W_DQ = H_D * 2 * HD_D
W_DK = H_D * 2 * HD_D
W_DV = H_D * DV_D
N_IN = W_GQ + W_GK + W_GV + W_GG + W_GA + W_DQ + W_DK + W_DV
SPLITS = (W_GQ,
          W_GQ + W_GK,
          W_GQ + W_GK + W_GV,
          W_GQ + W_GK + W_GV + W_GG,
          W_GQ + W_GK + W_GV + W_GG + W_GA,
          W_GQ + W_GK + W_GV + W_GG + W_GA + W_DQ,
          W_GQ + W_GK + W_GV + W_GG + W_GA + W_DQ + W_DK)

kernel_name = "hymba_gla_diffattn_stream_step"


def rmsnorm(x, g):
    xf = x.astype(jnp.float32)
    r = lax.rsqrt(jnp.mean(xf * xf, axis=-1, keepdims=True) + EPS)
    return (xf * r * g.astype(jnp.float32)).astype(x.dtype)


def rope(x, pos):
    half = ROT_DIM // 2
    inv = ROPE_THETA ** (-jnp.arange(0, ROT_DIM, 2, dtype=jnp.float32) / ROT_DIM)
    ang = pos.astype(jnp.float32)[:, None] * inv[None, :]
    cos = jnp.cos(ang)[None, :, None, :]
    sin = jnp.sin(ang)[None, :, None, :]
    x1 = x[..., :half].astype(jnp.float32)
    x2 = x[..., half:ROT_DIM].astype(jnp.float32)
    rot = jnp.concatenate([x1 * cos - x2 * sin, x2 * cos + x1 * sin], axis=-1)
    return jnp.concatenate([rot.astype(x.dtype), x[..., ROT_DIM:]], axis=-1)


def gla_chunk(S, inp):
    q, k, v, la = inp
    C = q.shape[1]
    b = jnp.cumsum(la, axis=1)
    qe = q * jnp.exp(b)
    ke = k * jnp.exp(-b)
    tri = jnp.tril(jnp.ones((C, C), dtype=bool))
    a = jnp.where(tri, jnp.einsum('bihk,bjhk->bhij', qe, ke), 0.0)
    o = jnp.einsum('bihk,bhkv->bihv', qe, S) + jnp.einsum('bhij,bjhv->bihv', a, v)
    bl = b[:, -1]
    S_new = jnp.exp(bl)[..., None] * S + jnp.einsum(
        'bjhk,bjhv->bhkv', k * jnp.exp(bl[:, None] - b), v)
    return S_new, o


def gla_scan(q, k, v, la, S0):
    B, S = q.shape[:2]
    C = min(S, CHUNK)
    nc = S // C

    def to_blocks(t):
        return jnp.moveaxis(t.reshape(B, nc, C, *t.shape[2:]).astype(jnp.float32), 1, 0)

    S_end, o = lax.scan(gla_chunk, S0.astype(jnp.float32),
                        (to_blocks(q), to_blocks(k), to_blocks(v), to_blocks(la)))
    o = jnp.moveaxis(o, 0, 1).reshape(B, S, H_G, DV_G)
    return o.astype(v.dtype), S_end.astype(S0.dtype)


def diff_attn_block(qb, qpos, k, v, kpos, lam):
    s = jnp.einsum('bqhcd,bkhcd->bhcqk', qb, k,
                   preferred_element_type=jnp.float32) * (HD_D ** -0.5)
    mask = (kpos // CHUNK)[None, :] <= (qpos // CHUNK)[:, None]
    s = jnp.where(mask[None, None, None], s, -jnp.inf)
    p = jax.nn.softmax(s, axis=-1)
    a = p[:, :, 0] - lam * p[:, :, 1]
    return jnp.einsum('bhqk,bkhv->bqhv', a.astype(v.dtype), v)


def diff_attention(q, k, v, qpos, kpos, lam):
    B, S = q.shape[:2]
    qb = min(S, Q_BLOCK)
    nb = S // qb
    qs = jnp.moveaxis(q.reshape(B, nb, qb, H_D, 2, HD_D), 1, 0)
    ps = qpos.reshape(nb, qb)
    o = lax.map(lambda t: diff_attn_block(t[0], t[1], k, v, kpos, lam), (qs, ps))
    return jnp.moveaxis(o, 0, 1).reshape(B, S, H_D, DV_D)


def mixer_layer(x, pos, S0, k_past, v_past, l, norm_l, w_in_l, w_gate_l, b_gate_l,
                gla_gain_l, lam_l, subln_l, w_out_l):
    B, S, _ = x.shape
    xn = rmsnorm(x, norm_l)
    z = xn @ w_in_l
    gq, gk, gv, gg, ga, dq, dk, dv = jnp.split(z, SPLITS, axis=-1)
    q = gq.reshape(B, S, H_G, DK_G) * (DK_G ** -0.5)
    k = gk.reshape(B, S, H_G, DK_G)
    v = gv.reshape(B, S, H_G, DV_G)
    la = jax.nn.log_sigmoid((ga @ w_gate_l + b_gate_l).astype(jnp.float32)) / GATE_TAU
    la = la.reshape(B, S, H_G, DK_G)
    o_g, S_new = gla_scan(q, k, v, la, S0)
    y_g = (rmsnorm(o_g, gla_gain_l) * jax.nn.silu(gg).reshape(B, S, H_G, DV_G)
           ).reshape(B, S, D_GLA)
    qd = rope(dq.reshape(B, S, 2 * H_D, HD_D), pos).reshape(B, S, H_D, 2, HD_D)
    kd = rope(dk.reshape(B, S, 2 * H_D, HD_D), pos).reshape(B, S, H_D, 2, HD_D)
    vd = dv.reshape(B, S, H_D, DV_D)
    if k_past is None:
        k_all, v_all, kpos = kd, vd, pos
    else:
        k_all = jnp.concatenate([k_past, kd], axis=1)
        v_all = jnp.concatenate([v_past, vd], axis=1)
        kpos = jnp.arange(k_past.shape[1] + S)
    lam_init = 0.8 - 0.6 * math.exp(-0.3 * l)
    lf = lam_l.astype(jnp.float32)
    lam = jnp.exp(jnp.sum(lf[0] * lf[1])) - jnp.exp(jnp.sum(lf[2] * lf[3])) + lam_init
    o_d = diff_attention(qd, k_all, v_all, pos, kpos, lam)
    y_d = (rmsnorm(o_d, subln_l) * (1.0 - lam_init)).reshape(B, S, D_DIFF)
    x = x + jnp.concatenate([y_g, y_d], axis=-1) @ w_out_l
    return x, S_new, kd, vd


def mlp_layer(x, norm_l, w_up_l, w_down_l):
    h = rmsnorm(x, norm_l)
    return x + jnp.square(jax.nn.relu(h @ w_up_l)) @ w_down_l


def setup_inputs(seed: int = 0) -> dict:
    key = jax.random.key(seed)
    ks = jax.random.split(key, 18)

    def nrm(k, shape, scale):
        return jax.random.normal(k, shape, jnp.float32) * scale

    return {
        "x_prompt": nrm(ks[0], (BATCH, SEQ, D_MODEL), 1.0),
        "x_sample": nrm(ks[1], (DEC_BATCH, DEC_SEQ, D_MODEL), 1.0),
        "cache_k": nrm(ks[2], (DEPTH, DEC_BATCH, PAST_LEN, H_D, 2, HD_D), 1.0),
        "cache_v": nrm(ks[3], (DEPTH, DEC_BATCH, PAST_LEN, H_D, DV_D), 1.0),
        "state_gla": nrm(ks[4], (DEPTH, DEC_BATCH, H_G, DK_G, DV_G), 1.0),
        "norm_mix": 1.0 + nrm(ks[5], (DEPTH, D_MODEL), 0.01),
        "w_in": nrm(ks[6], (DEPTH, D_MODEL, N_IN), D_MODEL ** -0.5),
        "w_gate": nrm(ks[7], (DEPTH, GATE_RANK, H_G * DK_G), GATE_RANK ** -0.5),
        "b_gate": nrm(ks[8], (DEPTH, H_G * DK_G), 0.1),
        "gla_gain": 1.0 + nrm(ks[9], (DEPTH, DV_G), 0.01),
        "diff_lambda": nrm(ks[10], (DEPTH, 4, HD_D), 0.1),
        "diff_subln": 1.0 + nrm(ks[11], (DEPTH, DV_D), 0.01),
        "w_out": nrm(ks[12], (DEPTH, D_MIX, D_MODEL), D_MIX ** -0.5),
        "norm_mlp": 1.0 + nrm(ks[13], (DEPTH, D_MODEL), 0.01),
        "w_up": nrm(ks[14], (DEPTH, D_MODEL, D_FF), D_MODEL ** -0.5),
        "w_down": nrm(ks[15], (DEPTH, D_FF, D_MODEL), D_FF ** -0.5),
        "norm_final": 1.0 + nrm(ks[16], (D_MODEL,), 0.01),
    }


def reference(x_prompt, x_sample, cache_k, cache_v, state_gla, norm_mix, w_in, w_gate,
              b_gate, gla_gain, diff_lambda, diff_subln, w_out, norm_mlp, w_up, w_down,
              norm_final):
    Bp, Sp = x_prompt.shape[:2]
    Bs, Ss = x_sample.shape[:2]
    past = cache_k.shape[2]
    pos_p = jnp.arange(Sp)
    pos_s = past + jnp.arange(Ss)
    S0_p = jnp.zeros((Bp, H_G, DK_G, DV_G), x_prompt.dtype)
    y_p, y_s = x_prompt, x_sample
    kp, vp, sp, kss, vss, sss = [], [], [], [], [], []
    for l in range(DEPTH):
        lw = (norm_mix[l], w_in[l], w_gate[l], b_gate[l], gla_gain[l], diff_lambda[l],
              diff_subln[l], w_out[l])
        y_p, s_new, k_new, v_new = mixer_layer(y_p, pos_p, S0_p, None, None, l, *lw)
        y_p = mlp_layer(y_p, norm_mlp[l], w_up[l], w_down[l])
        kp.append(k_new); vp.append(v_new); sp.append(s_new)
        y_s, s_new, k_new, v_new = mixer_layer(y_s, pos_s, state_gla[l], cache_k[l],
                                               cache_v[l], l, *lw)
        y_s = mlp_layer(y_s, norm_mlp[l], w_up[l], w_down[l])
        kss.append(k_new); vss.append(v_new); sss.append(s_new)
    y_p = rmsnorm(y_p, norm_final)
    y_s = rmsnorm(y_s, norm_final)
    return (y_p, y_s, jnp.stack(kp), jnp.stack(vp), jnp.stack(sp),
            jnp.stack(kss), jnp.stack(vss), jnp.stack(sss))
```

```python
import functools
import math

import jax
import jax.numpy as jnp
from jax import lax
from jax.experimental import pallas as pl
from jax.experimental.pallas import tpu as pltpu

CHUNK = 64
ROPE_THETA = 500000.0
GATE_TAU = 16.0
EPS = 1e-5
LANES = 128
VMEM_LIMIT = 56 * 1024 * 1024

F32 = jnp.float32
BF16 = jnp.bfloat16
_NT = (((1,), (1,)), ((), ()))
_TN = (((0,), (0,)), ((), ()))


def _tile(dim, pref):
    if dim <= pref:
        return dim
    t = pref
    while dim % t:
        t //= 2
    return t


def _params(*sem):
    return pltpu.CompilerParams(dimension_semantics=sem, vmem_limit_bytes=VMEM_LIMIT)


def _rmsnorm_kernel(x_ref, g_ref, o_ref):
    x = x_ref[...]
    r = lax.rsqrt(jnp.mean(x * x, axis=-1, keepdims=True) + EPS)
    o_ref[...] = (x * r * g_ref[...]).astype(o_ref.dtype)


def _rmsnorm(x, g_all, layer, out_dtype):
    m, d = x.shape
    tm = _tile(m, 512)
    return pl.pallas_call(
        _rmsnorm_kernel,
        grid=(m // tm,),
        in_specs=[pl.BlockSpec((tm, d), lambda i: (i, 0)),
                  pl.BlockSpec((None, 1, d), lambda i: (layer, 0, 0))],
        out_specs=pl.BlockSpec((tm, d), lambda i: (i, 0)),
        out_shape=jax.ShapeDtypeStruct((m, d), out_dtype),
        compiler_params=_params("parallel"),
        name="rmsnorm",
    )(x, g_all)


def _proj_kernel(*refs, rope, scale, act, n_out):
    a_ref, w_ref = refs[0], refs[1]
    outs = refs[len(refs) - n_out:]
    z = jnp.dot(a_ref[...], w_ref[...], preferred_element_type=F32)
    if rope:
        cos_ref, sa_ref, sb_ref = refs[2], refs[3], refs[4]
        tn = z.shape[1]
        reps = tn // LANES
        cos = jnp.tile(cos_ref[...], (1, reps))
        sa = jnp.tile(sa_ref[...], (1, reps))
        sb = jnp.tile(sb_ref[...], (1, reps))
        z = z * cos + pltpu.roll(z, 16, 1) * sa + pltpu.roll(z, tn - 16, 1) * sb
    if scale != 1.0:
        z = z * scale
    if act:
        z = jnp.square(jnp.maximum(z, 0.0))
    for o in outs:
        o[...] = z.astype(o.dtype)


def _proj(a, w_all, layer, col0, ncols, out_dtypes, *, rope_tabs=None, rows_per_seq=None,
          scale=1.0, act=False, name="proj"):
    m, k = a.shape
    tm = _tile(m, 1024)
    tn = _tile(math.gcd(col0, ncols), 512 if rope_tabs is not None else 1024)
    jb = col0 // tn
    in_specs = [pl.BlockSpec((tm, k), lambda i, j: (i, 0)),
                pl.BlockSpec((None, k, tn), lambda i, j: (layer, 0, j + jb))]
    args = [a, w_all]
    if rope_tabs is not None:
        assert rows_per_seq % tm == 0 or tm % rows_per_seq == 0
        if tm >= rows_per_seq:
            tabs = [jnp.tile(t, (tm // rows_per_seq, 1)) for t in rope_tabs]
            tmap = lambda i, j: (0, 0)
        else:
            tabs = list(rope_tabs)
            nblk = rows_per_seq // tm
            tmap = lambda i, j: (i % nblk, 0)
        in_specs += [pl.BlockSpec((tm, LANES), tmap)] * 3
        args += tabs
    n_out = len(out_dtypes)
    outs = pl.pallas_call(
        functools.partial(_proj_kernel, rope=rope_tabs is not None, scale=scale, act=act,
                          n_out=n_out),
        grid=(m // tm, ncols // tn),
        in_specs=in_specs,
        out_specs=[pl.BlockSpec((tm, tn), lambda i, j: (i, j))] * n_out,
        out_shape=[jax.ShapeDtypeStruct((m, ncols), dt) for dt in out_dtypes],
        compiler_params=_params("parallel", "parallel"),
        name=name,
    )(*args)
    return outs


def _outproj_kernel(yg_ref, yd_ref, w_ref, res_ref, o_ref):
    kg = yg_ref.shape[1]
    z = jnp.dot(yg_ref[...], w_ref[:kg, :], preferred_element_type=F32)
    z = z + jnp.dot(yd_ref[...], w_ref[kg:, :], preferred_element_type=F32)
    o_ref[...] = res_ref[...] + z


def _outproj(yg, yd, w_all, layer, res):
    m, kg = yg.shape
    kd = yd.shape[1]
    n = w_all.shape[2]
    tm = _tile(m, 1024)
    tn = _tile(n, 1024)
    return pl.pallas_call(
        _outproj_kernel,
        grid=(m // tm, n // tn),
        in_specs=[pl.BlockSpec((tm, kg), lambda i, j: (i, 0)),
                  pl.BlockSpec((tm, kd), lambda i, j: (i, 0)),
                  pl.BlockSpec((None, kg + kd, tn), lambda i, j: (layer, 0, j)),
                  pl.BlockSpec((tm, tn), lambda i, j: (i, j))],
        out_specs=pl.BlockSpec((tm, tn), lambda i, j: (i, j)),
        out_shape=jax.ShapeDtypeStruct((m, n), F32),
        compiler_params=_params("parallel", "parallel"),
        name="outproj",
    )(yg, yd, w_all, res)


def _down_kernel(h_ref, w_ref, res_ref, o_ref):
    z = jnp.dot(h_ref[...], w_ref[...], preferred_element_type=F32)
    kk = pl.program_id(2)

    @pl.when(kk == 0)
    def _():
        o_ref[...] = res_ref[...] + z

    @pl.when(kk > 0)
    def _():
        o_ref[...] += z


def _down(h, w_all, layer, res):
    m, k = h.shape
    n = w_all.shape[2]
    tm = _tile(m, 1024)
    tn = _tile(n, 1024)
    tk = _tile(k, 2048)
    return pl.pallas_call(
        _down_kernel,
        grid=(m // tm, n // tn, k // tk),
        in_specs=[pl.BlockSpec((tm, tk), lambda i, j, kk: (i, kk)),
                  pl.BlockSpec((None, tk, tn), lambda i, j, kk: (layer, kk, j)),
                  pl.BlockSpec((tm, tn), lambda i, j, kk: (i, j))],
        out_specs=pl.BlockSpec((tm, tn), lambda i, j, kk: (i, j)),
        out_shape=jax.ShapeDtypeStruct((m, n), F32),
        compiler_params=_params("parallel", "parallel", "arbitrary"),
        name="mlp_down",
    )(h, w_all, res)


def _gla_kernel(q_ref, k_ref, v_ref, gg_ref, ga_ref, wg_ref, bg_ref, gain_ref, s0_ref,
                y_ref, sout_ref, st_ref, *, chunk, q_scale):
    t = pl.program_id(2)
    nchunk = q_ref.shape[0] // chunk

    @pl.when(t == 0)
    def _():
        st_ref[...] = s0_ref[...].T

    row = lax.broadcasted_iota(jnp.int32, (chunk, chunk), 0)
    col = lax.broadcasted_iota(jnp.int32, (chunk, chunk), 1)
    tri = row >= col
    tri_f = tri.astype(F32)

    def body(c, carry):
        rows = pl.ds(pl.multiple_of(c * chunk, chunk), chunk)
        q = q_ref[rows, :] * q_scale
        k = k_ref[rows, :]
        v = v_ref[rows, :]
        pre = jnp.dot(ga_ref[rows, :], wg_ref[...], preferred_element_type=F32,
                      precision=lax.Precision.HIGHEST) + bg_ref[...]
        la = jax.nn.log_sigmoid(pre) / GATE_TAU
        b = jnp.dot(tri_f, la, preferred_element_type=F32,
                    precision=lax.Precision.HIGHEST)
        qe = (q * jnp.exp(b)).astype(BF16)
        ke = (k * jnp.exp(-b)).astype(BF16)
        a = lax.dot_general(qe, ke, _NT, preferred_element_type=F32)
        a = jnp.where(tri, a, 0.0)
        st = st_ref[...]
        o = lax.dot_general(qe, st.astype(BF16), _NT, preferred_element_type=F32)
        o = o + jnp.dot(a.astype(BF16), v, preferred_element_type=F32)
        bl = b[chunk - 1:chunk, :]
        kd = (k * jnp.exp(bl - b)).astype(BF16)
        st_ref[...] = st * jnp.exp(bl) + lax.dot_general(v, kd, _TN, preferred_element_type=F32)
        r = lax.rsqrt(jnp.mean(o * o, axis=-1, keepdims=True) + EPS)
        gg = gg_ref[rows, :].astype(F32)
        y = (o * r * gain_ref[...]) * (gg * jax.nn.sigmoid(gg))
        y_ref[rows, :] = y.astype(y_ref.dtype)
        return carry

    lax.fori_loop(0, nchunk, body, 0)

    @pl.when(t == pl.num_programs(2) - 1)
    def _():
        sout_ref[...] = st_ref[...].T


def _gla(qk, vg, ga, wg_all, bg_all, gain_all, s0_all, layer, s0_layer, nb, seq, nh):
    m = qk.shape[0]
    dk = qk.shape[1] // (2 * nh)
    dv = vg.shape[1] // (2 * nh)
    chunk = min(seq, CHUNK)
    tt = _tile(seq, 1024)
    nt = seq // tt
    rmap = lambda b, h, t: (b * nt + t, h)
    rmap2 = lambda b, h, t: (b * nt + t, h + nh)
    y, s_out = pl.pallas_call(
        functools.partial(_gla_kernel, chunk=chunk, q_scale=dk ** -0.5),
        grid=(nb, nh, nt),
        in_specs=[pl.BlockSpec((tt, dk), rmap),
                  pl.BlockSpec((tt, dk), rmap2),
                  pl.BlockSpec((tt, dv), rmap),
                  pl.BlockSpec((tt, dv), rmap2),
                  pl.BlockSpec((tt, LANES), lambda b, h, t: (b * nt + t, 0)),
                  pl.BlockSpec((None, LANES, dk), lambda b, h, t: (layer, 0, h)),
                  pl.BlockSpec((None, 1, dk), lambda b, h, t: (layer, 0, h)),
                  pl.BlockSpec((None, 1, dv), lambda b, h, t: (layer, 0, 0)),
                  pl.BlockSpec((None, None, None, dk, dv),
                               lambda b, h, t: (s0_layer, b, h, 0, 0))],
        out_specs=[pl.BlockSpec((tt, dv), rmap),
                   pl.BlockSpec((None, None, dk, dv), lambda b, h, t: (b, h, 0, 0))],
        out_shape=[jax.ShapeDtypeStruct((m, nh * dv), BF16),
                   jax.ShapeDtypeStruct((nb, nh, dk, dv), F32)],
        scratch_shapes=[pltpu.VMEM((dv, dk), F32)],
        compiler_params=_params("parallel", "parallel", "arbitrary"),
        name="gla",
    )(qk, qk, vg, vg, ga, wg_all, bg_all, gain_all, s0_all)
    return y, s_out


def _flash_update(q, k, v, mask, m_ref, l_ref, acc_ref):
    hd = q.shape[1] // 2
    for c in range(2):
        s = lax.dot_general(q[:, c * hd:(c + 1) * hd], k[:, c * hd:(c + 1) * hd], _NT,
                            preferred_element_type=F32)
        if mask is not None:
            s = jnp.where(mask, s, -jnp.inf)
        m_prev = m_ref[c]
        m_new = jnp.maximum(m_prev, jnp.max(s, axis=-1, keepdims=True))
        alpha = jnp.exp(m_prev - m_new)
        p = jnp.exp(s - m_new)
        l_ref[c] = alpha * l_ref[c] + jnp.sum(p, axis=-1, keepdims=True)
        acc_ref[c] = alpha * acc_ref[c] + jnp.dot(p.astype(BF16), v, preferred_element_type=F32)
        m_ref[c] = m_new


def _flash_init(m_ref, l_ref, acc_ref):
    m_ref[...] = jnp.full(m_ref.shape, -jnp.inf, F32)
    l_ref[...] = jnp.zeros(l_ref.shape, F32)
    acc_ref[...] = jnp.zeros(acc_ref.shape, F32)


def _flash_finish(lam_ref, sub_ref, o_ref, l_ref, acc_ref, lam_init):
    lp = lam_ref[...]
    lam = (jnp.exp(jnp.sum(lp[0:1] * lp[1:2], axis=-1, keepdims=True))
           - jnp.exp(jnp.sum(lp[2:3] * lp[3:4], axis=-1, keepdims=True)) + lam_init)
    o = acc_ref[0] / l_ref[0] - lam * (acc_ref[1] / l_ref[1])
    r = lax.rsqrt(jnp.mean(o * o, axis=-1, keepdims=True) + EPS)
    o_ref[...] = ((o * r * sub_ref[...]) * (1.0 - lam_init)).astype(o_ref.dtype)


def _attn_prompt_kernel(lam_ref, sub_ref, q_ref, k_ref, v_ref, o_ref, m_ref, l_ref, acc_ref,
                        *, lam_init):
    qi = pl.program_id(2)
    ki = pl.program_id(3)
    tq = q_ref.shape[0]
    tk = k_ref.shape[0]

    @pl.when(ki == 0)
    def _():
        _flash_init(m_ref, l_ref, acc_ref)

    kend = ((qi * tq + tq - 1) // CHUNK + 1) * CHUNK

    @pl.when(ki * tk < kend)
    def _():
        qpos = qi * tq + lax.broadcasted_iota(jnp.int32, (tq, tk), 0)
        kpos = ki * tk + lax.broadcasted_iota(jnp.int32, (tq, tk), 1)
        mask = (kpos // CHUNK) <= (qpos // CHUNK)
        _flash_update(q_ref[...], k_ref[...], v_ref[...], mask, m_ref, l_ref, acc_ref)

    @pl.when(ki == pl.num_programs(3) - 1)
    def _():
        _flash_finish(lam_ref, sub_ref, o_ref, l_ref, acc_ref, lam_init)


def _attn_prompt(q, k, v, lam_all, sub_all, layer, lam_init, nb, seq, nh):
    m = q.shape[0]
    w = q.shape[1] // nh
    dv = v.shape[1] // nh
    tq = _tile(seq, 256)
    tk = _tile(seq, 256)
    nq, nk = seq // tq, seq // tk

    def kvmap(b, h, qi, ki):
        last = ((qi * tq + tq - 1) // CHUNK + 1) * CHUNK
        return (b * nk + jnp.minimum(ki, (last - 1) // tk), h)

    return pl.pallas_call(
        functools.partial(_attn_prompt_kernel, lam_init=lam_init),
        grid=(nb, nh, nq, nk),
        in_specs=[pl.BlockSpec((None, 4, w // 2), lambda b, h, qi, ki: (layer, 0, 0)),
                  pl.BlockSpec((None, 1, dv), lambda b, h, qi, ki: (layer, 0, 0)),
                  pl.BlockSpec((tq, w), lambda b, h, qi, ki: (b * nq + qi, h)),
                  pl.BlockSpec((tk, w), kvmap),
                  pl.BlockSpec((tk, dv), kvmap)],
        out_specs=pl.BlockSpec((tq, dv), lambda b, h, qi, ki: (b * nq + qi, h)),
        out_shape=jax.ShapeDtypeStruct((m, nh * dv), BF16),
        scratch_shapes=[pltpu.VMEM((2, tq, 1), F32), pltpu.VMEM((2, tq, 1), F32),
                        pltpu.VMEM((2, tq, dv), F32)],
        compiler_params=_params("parallel", "parallel", "parallel", "arbitrary"),
        name="diff_attn_prompt",
    )(lam_all, sub_all, q, k, v)


def _attn_sample_kernel(lam_ref, sub_ref, q_ref, kc_ref, vc_ref, kn_ref, vn_ref, o_ref,
                        m_ref, l_ref, acc_ref, *, lam_init, past):
    ki = pl.program_id(2)
    nc = pl.num_programs(2) - 1
    tq = q_ref.shape[0]
    tk = kc_ref.shape[0]
    tn = kn_ref.shape[0]

    @pl.when(ki == 0)
    def _():
        _flash_init(m_ref, l_ref, acc_ref)

    qchunk = (past + lax.broadcasted_iota(jnp.int32, (tq, 1), 0)) // CHUNK

    @pl.when(ki < nc)
    def _():
        kpos = ki * tk + lax.broadcasted_iota(jnp.int32, (tq, tk), 1)
        mask = (kpos // CHUNK) <= qchunk
        _flash_update(q_ref[...], kc_ref[...].astype(BF16), vc_ref[...].astype(BF16), mask,
                      m_ref, l_ref, acc_ref)

    @pl.when(ki == nc)
    def _():
        kpos = past + lax.broadcasted_iota(jnp.int32, (tq, tn), 1)
        mask = (kpos // CHUNK) <= qchunk
        _flash_update(q_ref[...], kn_ref[...], vn_ref[...], mask, m_ref, l_ref, acc_ref)
        _flash_finish(lam_ref, sub_ref, o_ref, l_ref, acc_ref, lam_init)


def _attn_sample(q, kn, vn, kc_all, vc_all, lam_all, sub_all, layer, lam_init, nb, seq, nh):
    m = q.shape[0]
    w = q.shape[1] // nh
    dv = vn.shape[1] // nh
    past = kc_all.shape[1]
    tk = _tile(past, 512)
    nc = past // tk
    cmap = lambda b, h, ki: (layer * nb + b, jnp.minimum(ki, nc - 1), h)
    nmap = lambda b, h, ki: (b, h)
    return pl.pallas_call(
        functools.partial(_attn_sample_kernel, lam_init=lam_init, past=past),
        grid=(nb, nh, nc + 1),
        in_specs=[pl.BlockSpec((None, 4, w // 2), lambda b, h, ki: (layer, 0, 0)),
                  pl.BlockSpec((None, 1, dv), lambda b, h, ki: (layer, 0, 0)),
                  pl.BlockSpec((seq, w), nmap),
                  pl.BlockSpec((None, tk, w), cmap),
                  pl.BlockSpec((None, tk, dv), cmap),
                  pl.BlockSpec((seq, w), nmap),
                  pl.BlockSpec((seq, dv), nmap)],
        out_specs=pl.BlockSpec((seq, dv), nmap),
        out_shape=jax.ShapeDtypeStruct((m, nh * dv), BF16),
        scratch_shapes=[pltpu.VMEM((2, seq, 1), F32), pltpu.VMEM((2, seq, 1), F32),
                        pltpu.VMEM((2, seq, dv), F32)],
        compiler_params=_params("parallel", "parallel", "arbitrary"),
        name="diff_attn_sample",
    )(lam_all, sub_all, q, kc_all, vc_all, kn, vn)


def _rope_tables(pos, hd):
    rot = hd // 4
    half = rot // 2
    inv = ROPE_THETA ** (-jnp.arange(0, rot, 2, dtype=F32) / rot)
    ang = pos.astype(F32)[:, None] * inv[None, :]
    cos, sin = jnp.cos(ang), jnp.sin(ang)
    n = pos.shape[0]
    ones = jnp.ones((n, hd - rot), F32)
    zeros_h = jnp.zeros((n, half), F32)
    zeros_r = jnp.zeros((n, hd - rot), F32)
    tab_cos = jnp.concatenate([cos, cos, ones], axis=1)
    tab_sa = jnp.concatenate([zeros_h, sin, zeros_r], axis=1)
    tab_sb = jnp.concatenate([-sin, zeros_h, zeros_r], axis=1)
    return tab_cos, tab_sa, tab_sb


def kernel(x_prompt, x_sample, cache_k, cache_v, state_gla, norm_mix, w_in, w_gate, b_gate,
           gla_gain, diff_lambda, diff_subln, w_out, norm_mlp, w_up, w_down, norm_final):
    depth, d_model, _ = w_in.shape
    bp, sp, _ = x_prompt.shape
    bs, ss, _ = x_sample.shape
    _, _, past, h_d, _, hd_d = cache_k.shape
    _, _, h_g, dk_g, dv_g = state_gla.shape
    rank = w_gate.shape[1]
    dv_d = cache_v.shape[-1]
    w_gqk = 2 * h_g * dk_g
    w_gvg = 2 * h_g * dv_g
    w_dq = h_d * 2 * hd_d
    w_dv = h_d * dv_d
    assert hd_d == LANES and hd_d // 8 == 16 and rank <= LANES

    o_ga = w_gqk + w_gvg
    w_main = jnp.concatenate([w_in[:, :, :o_ga], w_in[:, :, o_ga + rank:]], axis=2).astype(BF16)
    w_ga = jnp.pad(w_in[:, :, o_ga:o_ga + rank], ((0, 0), (0, 0), (0, LANES - rank))).astype(BF16)
    w_gate_p = jnp.pad(w_gate, ((0, 0), (0, LANES - rank), (0, 0)))
    w_out_b = w_out.astype(BF16)
    w_up_b = w_up.astype(BF16)
    w_down_b = w_down.astype(BF16)
    norm_mix3 = norm_mix[:, None, :]
    norm_mlp3 = norm_mlp[:, None, :]
    b_gate3 = b_gate[:, None, :]
    gain3 = gla_gain[:, None, :]
    subln3 = diff_subln[:, None, :]
    kc_all = cache_k.reshape(depth * bs, past, w_dq)
    vc_all = cache_v.reshape(depth * bs, past, w_dv)
    zero_state = jnp.zeros((1, bp, h_g, dk_g, dv_g), F32)
    tabs_p = _rope_tables(jnp.arange(sp), hd_d)
    tabs_s = _rope_tables(past + jnp.arange(ss), hd_d)

    c_dq = w_gqk + w_gvg
    c_dk = c_dq + w_dq
    c_dv = c_dk + w_dq

    def layer_fn(x, l, nb, seq, tabs, s0_all, s0_layer, cached):
        lam_init = 0.8 - 0.6 * math.exp(-0.3 * l)
        xn = _rmsnorm(x, norm_mix3, l, BF16)
        (qk,) = _proj(xn, w_main, l, 0, w_gqk, [F32], name="proj_gqk")
        (vg,) = _proj(xn, w_main, l, w_gqk, w_gvg, [BF16], name="proj_gvg")
        (ga,) = _proj(xn, w_ga, l, 0, LANES, [F32], name="proj_ga")
        (qd,) = _proj(xn, w_main, l, c_dq, w_dq, [BF16], rope_tabs=tabs, rows_per_seq=seq,
                      scale=hd_d ** -0.5, name="proj_dq")
        kd32, kd16 = _proj(xn, w_main, l, c_dk, w_dq, [F32, BF16], rope_tabs=tabs,
                           rows_per_seq=seq, name="proj_dk")
        vd32, vd16 = _proj(xn, w_main, l, c_dv, w_dv, [F32, BF16], name="proj_dv")
        yg, s_new = _gla(qk, vg, ga, w_gate_p, b_gate3, gain3, s0_all, l, s0_layer, nb, seq, h_g)
        if cached:
            yd = _attn_sample(qd, kd16, vd16, kc_all, vc_all, diff_lambda, subln3, l, lam_init,
                              nb, seq, h_d)
        else:
            yd = _attn_prompt(qd, kd16, vd16, diff_lambda, subln3, l, lam_init, nb, seq, h_d)
        x = _outproj(yg, yd, w_out_b, l, x)
        hn = _rmsnorm(x, norm_mlp3, l, BF16)
        (h,) = _proj(hn, w_up_b, l, 0, w_up_b.shape[2], [BF16], act=True, name="mlp_up")
        x = _down(h, w_down_b, l, x)
        return (x, kd32.reshape(nb, seq, h_d, 2, hd_d), vd32.reshape(nb, seq, h_d, dv_d), s_new)

    y_p = x_prompt.reshape(bp * sp, d_model)
    y_s = x_sample.reshape(bs * ss, d_model)
    kp, vp, stp, kss, vss, sts = [], [], [], [], [], []
    for l in range(depth):
        y_p, k_new, v_new, s_new = layer_fn(y_p, l, bp, sp, tabs_p, zero_state, 0, False)
        kp.append(k_new); vp.append(v_new); stp.append(s_new)
        y_s, k_new, v_new, s_new = layer_fn(y_s, l, bs, ss, tabs_s, state_gla, l, True)
        kss.append(k_new); vss.append(v_new); sts.append(s_new)
    nf3 = norm_final[None, None, :]
    y_p = _rmsnorm(y_p, nf3, 0, F32).reshape(bp, sp, d_model)
    y_s = _rmsnorm(y_s, nf3, 0, F32).reshape(bs, ss, d_model)
    return (y_p, y_s, jnp.stack(kp), jnp.stack(vp), jnp.stack(stp),
            jnp.stack(kss), jnp.stack(vss), jnp.stack(sts))
```

```python
import functools
import math

import jax
import jax.numpy as jnp
from jax import lax
from jax.experimental import pallas as pl
from jax.experimental.pallas import tpu as pltpu

CHUNK = 64
ROPE_THETA = 500000.0
GATE_TAU = 16.0
EPS = 1e-5
LANES = 128
VMEM_LIMIT = 56 * 1024 * 1024

F32 = jnp.float32
BF16 = jnp.bfloat16
_NT = (((1,), (1,)), ((), ()))
_TN = (((0,), (0,)), ((), ()))


def _tile(dim, pref):
    if dim <= pref:
        return dim
    t = pref
    while dim % t:
        t //= 2
    return t


def _params(*sem):
    return pltpu.CompilerParams(dimension_semantics=sem, vmem_limit_bytes=VMEM_LIMIT)


def _rmsnorm_kernel(x_ref, g_ref, o_ref):
    x = x_ref[...]
    r = lax.rsqrt(jnp.mean(x * x, axis=-1, keepdims=True) + EPS)
    o_ref[...] = (x * r * g_ref[...]).astype(o_ref.dtype)


def _rmsnorm(x, g_all, layer, out_dtype):
    m, d = x.shape
    tm = _tile(m, 512)
    return pl.pallas_call(
        _rmsnorm_kernel,
        grid=(m // tm,),
        in_specs=[pl.BlockSpec((tm, d), lambda i: (i, 0)),
                  pl.BlockSpec((None, 1, d), lambda i: (layer, 0, 0))],
        out_specs=pl.BlockSpec((tm, d), lambda i: (i, 0)),
        out_shape=jax.ShapeDtypeStruct((m, d), out_dtype),
        compiler_params=_params("parallel"),
        name="rmsnorm",
    )(x, g_all)


def _proj_kernel(*refs, rope, scale, act, n_out):
    a_ref, w_ref = refs[0], refs[1]
    outs = refs[len(refs) - n_out:]
    z = jnp.dot(a_ref[...], w_ref[...], preferred_element_type=F32)
    if rope:
        cos_ref, sa_ref, sb_ref = refs[2], refs[3], refs[4]
        tn = z.shape[1]
        reps = tn // LANES
        cos = jnp.tile(cos_ref[...], (1, reps))
        sa = jnp.tile(sa_ref[...], (1, reps))
        sb = jnp.tile(sb_ref[...], (1, reps))
        z = z * cos + pltpu.roll(z, 16, 1) * sa + pltpu.roll(z, tn - 16, 1) * sb
    if scale != 1.0:
        z = z * scale
    if act:
        z = jnp.square(jnp.maximum(z, 0.0))
    for o in outs:
        o[...] = z.astype(o.dtype)


def _proj(a, w_all, layer, col0, ncols, out_dtypes, *, rope_tabs=None, rows_per_seq=None,
          scale=1.0, act=False, name="proj"):
    m, k = a.shape
    tm = _tile(m, 1024)
    tn = _tile(math.gcd(col0, ncols), 512 if rope_tabs is not None else 1024)
    jb = col0 // tn
    in_specs = [pl.BlockSpec((tm, k), lambda i, j: (i, 0)),
                pl.BlockSpec((None, k, tn), lambda i, j: (layer, 0, j + jb))]
    args = [a, w_all]
    if rope_tabs is not None:
        assert rows_per_seq % tm == 0 or tm % rows_per_seq == 0
        if tm >= rows_per_seq:
            tabs = [jnp.tile(t, (tm // rows_per_seq, 1)) for t in rope_tabs]
            tmap = lambda i, j: (0, 0)
        else:
            tabs = list(rope_tabs)
            nblk = rows_per_seq // tm
            tmap = lambda i, j: (i % nblk, 0)
        in_specs += [pl.BlockSpec((tm, LANES), tmap)] * 3
        args += tabs
    n_out = len(out_dtypes)
    outs = pl.pallas_call(
        functools.partial(_proj_kernel, rope=rope_tabs is not None, scale=scale, act=act,
                          n_out=n_out),
        grid=(m // tm, ncols // tn),
        in_specs=in_specs,
        out_specs=[pl.BlockSpec((tm, tn), lambda i, j: (i, j))] * n_out,
        out_shape=[jax.ShapeDtypeStruct((m, ncols), dt) for dt in out_dtypes],
        compiler_params=_params("parallel", "parallel"),
        name=name,
    )(*args)
    return outs


def _outproj_kernel(yg_ref, yd_ref, w_ref, res_ref, o_ref):
    kg = yg_ref.shape[1]
    z = jnp.dot(yg_ref[...], w_ref[:kg, :], preferred_element_type=F32)
    z = z + jnp.dot(yd_ref[...], w_ref[kg:, :], preferred_element_type=F32)
    o_ref[...] = res_ref[...] + z


def _outproj(yg, yd, w_all, layer, res):
    m, kg = yg.shape
    kd = yd.shape[1]
    n = w_all.shape[2]
    tm = _tile(m, 1024)
    tn = _tile(n, 1024)
    return pl.pallas_call(
        _outproj_kernel,
        grid=(m // tm, n // tn),
        in_specs=[pl.BlockSpec((tm, kg), lambda i, j: (i, 0)),
                  pl.BlockSpec((tm, kd), lambda i, j: (i, 0)),
                  pl.BlockSpec((None, kg + kd, tn), lambda i, j: (layer, 0, j)),
                  pl.BlockSpec((tm, tn), lambda i, j: (i, j))],
        out_specs=pl.BlockSpec((tm, tn), lambda i, j: (i, j)),
        out_shape=jax.ShapeDtypeStruct((m, n), F32),
        compiler_params=_params("parallel", "parallel"),
        name="outproj",
    )(yg, yd, w_all, res)


def _down_kernel(h_ref, w_ref, res_ref, o_ref):
    z = jnp.dot(h_ref[...], w_ref[...], preferred_element_type=F32)
    kk = pl.program_id(2)

    @pl.when(kk == 0)
    def _():
        o_ref[...] = res_ref[...] + z

    @pl.when(kk > 0)
    def _():
        o_ref[...] += z


def _down(h, w_all, layer, res):
    m, k = h.shape
    n = w_all.shape[2]
    tm = _tile(m, 1024)
    tn = _tile(n, 1024)
    tk = _tile(k, 2048)
    return pl.pallas_call(
        _down_kernel,
        grid=(m // tm, n // tn, k // tk),
        in_specs=[pl.BlockSpec((tm, tk), lambda i, j, kk: (i, kk)),
                  pl.BlockSpec((None, tk, tn), lambda i, j, kk: (layer, kk, j)),
                  pl.BlockSpec((tm, tn), lambda i, j, kk: (i, j))],
        out_specs=pl.BlockSpec((tm, tn), lambda i, j, kk: (i, j)),
        out_shape=jax.ShapeDtypeStruct((m, n), F32),
        compiler_params=_params("parallel", "parallel", "arbitrary"),
        name="mlp_down",
    )(h, w_all, res)


def _split2(x):
    hi = x.astype(BF16)
    lo = (x - hi.astype(F32)).astype(BF16)
    return hi, lo


def _gla_kernel(q_ref, k_ref, v_ref, gg_ref, ga_ref, wg_ref, bg_ref, gain_ref, s0_ref,
                y_ref, sout_ref, st_ref, la_ref, qe_ref, kd_ref, oi_ref, ebl_ref,
                *, chunk, q_scale):
    t = pl.program_id(2)
    nchunk = q_ref.shape[0] // chunk
    unroll = 2 if nchunk % 2 == 0 else 1

    @pl.when(t == 0)
    def _():
        st_ref[...] = s0_ref[...].T

    ga_hi, ga_lo = _split2(ga_ref[...])
    w_hi, w_lo = _split2(wg_ref[...])
    pre = (jnp.dot(ga_hi, w_hi, preferred_element_type=F32)
           + jnp.dot(ga_lo, w_hi, preferred_element_type=F32)
           + jnp.dot(ga_hi, w_lo, preferred_element_type=F32)) + bg_ref[...]
    la_ref[...] = jax.nn.log_sigmoid(pre) / GATE_TAU

    row = lax.broadcasted_iota(jnp.int32, (chunk, chunk), 0)
    col = lax.broadcasted_iota(jnp.int32, (chunk, chunk), 1)
    tri = row >= col
    tri_b = tri.astype(BF16)

    def local(c, carry):
        rows = pl.ds(pl.multiple_of(c * chunk, chunk), chunk)
        la_hi, la_lo = _split2(la_ref[rows, :])
        b = (jnp.dot(tri_b, la_hi, preferred_element_type=F32)
             + jnp.dot(tri_b, la_lo, preferred_element_type=F32))
        k = k_ref[rows, :]
        qe = (q_ref[rows, :] * q_scale * jnp.exp(b)).astype(BF16)
        ke = (k * jnp.exp(-b)).astype(BF16)
        a = lax.dot_general(qe, ke, _NT, preferred_element_type=F32)
        a = jnp.where(tri, a, 0.0)
        oi_ref[rows, :] = jnp.dot(a.astype(BF16), v_ref[rows, :], preferred_element_type=F32)
        bl = b[chunk - 1:chunk, :]
        qe_ref[rows, :] = qe
        kd_ref[rows, :] = (k * jnp.exp(bl - b)).astype(BF16)
        ebl_ref[pl.ds(c, 1), :] = jnp.exp(bl)
        return carry

    lax.fori_loop(0, nchunk, local, 0, unroll=unroll)

    def serial(c, carry):
        rows = pl.ds(pl.multiple_of(c * chunk, chunk), chunk)
        st = st_ref[...]
        o = lax.dot_general(qe_ref[rows, :], st.astype(BF16), _NT, preferred_element_type=F32)
        o = o + oi_ref[rows, :]
        st_ref[...] = st * ebl_ref[pl.ds(c, 1), :] + lax.dot_general(
            v_ref[rows, :], kd_ref[rows, :], _TN, preferred_element_type=F32)
        r = lax.rsqrt(jnp.mean(o * o, axis=-1, keepdims=True) + EPS)
        gg = gg_ref[rows, :].astype(F32)
        y = (o * r * gain_ref[...]) * (gg * jax.nn.sigmoid(gg))
        y_ref[rows, :] = y.astype(y_ref.dtype)
        return carry

    lax.fori_loop(0, nchunk, serial, 0, unroll=unroll)

    @pl.when(t == pl.num_programs(2) - 1)
    def _():
        sout_ref[...] = st_ref[...].T


def _gla(qk, vg, ga, wg_all, bg_all, gain_all, s0_all, layer, s0_layer, nb, seq, nh):
    m = qk.shape[0]
    dk = qk.shape[1] // (2 * nh)
    dv = vg.shape[1] // (2 * nh)
    chunk = min(seq, CHUNK)
    tt = _tile(seq, 1024)
    nt = seq // tt
    rmap = lambda b, h, t: (b * nt + t, h)
    rmap2 = lambda b, h, t: (b * nt + t, h + nh)
    y, s_out = pl.pallas_call(
        functools.partial(_gla_kernel, chunk=chunk, q_scale=dk ** -0.5),
        grid=(nb, nh, nt),
        in_specs=[pl.BlockSpec((tt, dk), rmap),
                  pl.BlockSpec((tt, dk), rmap2),
                  pl.BlockSpec((tt, dv), rmap),
                  pl.BlockSpec((tt, dv), rmap2),
                  pl.BlockSpec((tt, LANES), lambda b, h, t: (b * nt + t, 0)),
                  pl.BlockSpec((None, LANES, dk), lambda b, h, t: (layer, 0, h)),
                  pl.BlockSpec((None, 1, dk), lambda b, h, t: (layer, 0, h)),
                  pl.BlockSpec((None, 1, dv), lambda b, h, t: (layer, 0, 0)),
                  pl.BlockSpec((None, None, None, dk, dv),
                               lambda b, h, t: (s0_layer, b, h, 0, 0))],
        out_specs=[pl.BlockSpec((tt, dv), rmap),
                   pl.BlockSpec((None, None, dk, dv), lambda b, h, t: (b, h, 0, 0))],
        out_shape=[jax.ShapeDtypeStruct((m, nh * dv), BF16),
                   jax.ShapeDtypeStruct((nb, nh, dk, dv), F32)],
        scratch_shapes=[pltpu.VMEM((dv, dk), F32),
                        pltpu.VMEM((tt, dk), F32),
                        pltpu.VMEM((tt, dk), BF16),
                        pltpu.VMEM((tt, dk), BF16),
                        pltpu.VMEM((tt, dv), F32),
                        pltpu.VMEM((max(tt // chunk, 8), dk), F32)],
        compiler_params=_params("parallel", "parallel", "arbitrary"),
        name="gla",
    )(qk, qk, vg, vg, ga, wg_all, bg_all, gain_all, s0_all)
    return y, s_out


def _softmax_av(score_parts, value_parts):
    m = functools.reduce(jnp.maximum,
                         [jnp.max(s, axis=-1, keepdims=True) for s in score_parts])
    ps = [jnp.exp2(s - m) for s in score_parts]
    l = functools.reduce(jnp.add, [jnp.sum(p, axis=-1, keepdims=True) for p in ps])
    acc = functools.reduce(jnp.add, [jnp.dot(p.astype(BF16), v, preferred_element_type=F32)
                                     for p, v in zip(ps, value_parts)])
    return acc * (1.0 / l)


def _diff_combine(heads, lam_ref, sub_ref, lam_init):
    lp = lam_ref[...]
    lam = (jnp.exp(jnp.sum(lp[0:1] * lp[1:2], axis=-1, keepdims=True))
           - jnp.exp(jnp.sum(lp[2:3] * lp[3:4], axis=-1, keepdims=True)) + lam_init)
    o = heads[0] - lam * heads[1]
    r = lax.rsqrt(jnp.mean(o * o, axis=-1, keepdims=True) + EPS)
    return (o * r * sub_ref[...]) * (1.0 - lam_init)


def _visibility_bias(qpos, kpos):
    return jnp.where((kpos // CHUNK) <= (qpos // CHUNK), 0.0, -jnp.inf).astype(F32)


def _attn_prompt_kernel(lam_ref, sub_ref, q_ref, k_ref, v_ref, o_ref, *, lam_init, tq):
    hd = q_ref.shape[1] // 2
    diag_bias = _visibility_bias(lax.broadcasted_iota(jnp.int32, (tq, tq), 0),
                                 lax.broadcasted_iota(jnp.int32, (tq, tq), 1))
    for qi in range(q_ref.shape[0] // tq):
        q0, q1 = qi * tq, (qi + 1) * tq
        heads = []
        for c in range(2):
            cols = slice(c * hd, (c + 1) * hd)
            qc = q_ref[q0:q1, cols]
            scores = [lax.dot_general(qc, k_ref[q0:q1, cols], _NT,
                                      preferred_element_type=F32) + diag_bias]
            values = [v_ref[q0:q1, :]]
            if qi:
                scores.append(lax.dot_general(qc, k_ref[0:q0, cols], _NT,
                                              preferred_element_type=F32))
                values.append(v_ref[0:q0, :])
            heads.append(_softmax_av(scores, values))
        o_ref[q0:q1, :] = _diff_combine(heads, lam_ref, sub_ref, lam_init).astype(o_ref.dtype)


def _attn_prompt(q, k, v, lam_all, sub_all, layer, lam_init, nb, seq, nh):
    m = q.shape[0]
    w = q.shape[1] // nh
    dv = v.shape[1] // nh
    tq = _tile(seq, 256)
    assert tq % CHUNK == 0
    bmap = lambda b, h: (b, h)
    return pl.pallas_call(
        functools.partial(_attn_prompt_kernel, lam_init=lam_init, tq=tq),
        grid=(nb, nh),
        in_specs=[pl.BlockSpec((None, 4, w // 2), lambda b, h: (layer, 0, 0)),
                  pl.BlockSpec((None, 1, dv), lambda b, h: (layer, 0, 0)),
                  pl.BlockSpec((seq, w), bmap),
                  pl.BlockSpec((seq, w), bmap),
                  pl.BlockSpec((seq, dv), bmap)],
        out_specs=pl.BlockSpec((seq, dv), bmap),
        out_shape=jax.ShapeDtypeStruct((m, nh * dv), BF16),
        compiler_params=_params("parallel", "parallel"),
        name="diff_attn_prompt",
    )(lam_all, sub_all, q, k, v)


def _attn_sample_kernel(lam_ref, sub_ref, q_ref, kc_ref, vc_ref, kn_ref, vn_ref, o_ref,
                        *, lam_init):
    tq = q_ref.shape[0]
    past = kc_ref.shape[0]
    tn = kn_ref.shape[0]
    hd = q_ref.shape[1] // 2
    qpos = past + lax.broadcasted_iota(jnp.int32, (tq, 1), 0)
    bias_c = _visibility_bias(qpos, lax.broadcasted_iota(jnp.int32, (tq, past), 1))
    bias_n = _visibility_bias(qpos, past + lax.broadcasted_iota(jnp.int32, (tq, tn), 1))
    values = [vc_ref[...].astype(BF16), vn_ref[...]]
    heads = []
    for c in range(2):
        cols = slice(c * hd, (c + 1) * hd)
        qc = q_ref[:, cols]
        scores = [lax.dot_general(qc, kc_ref[:, cols].astype(BF16), _NT,
                                  preferred_element_type=F32) + bias_c,
                  lax.dot_general(qc, kn_ref[:, cols], _NT,
                                  preferred_element_type=F32) + bias_n]
        heads.append(_softmax_av(scores, values))
    o_ref[...] = _diff_combine(heads, lam_ref, sub_ref, lam_init).astype(o_ref.dtype)


def _attn_sample(q, kn, vn, kc_all, vc_all, lam_all, sub_all, layer, lam_init, nb, seq, nh):
    m = q.shape[0]
    w = q.shape[1] // nh
    dv = vn.shape[1] // nh
    past = kc_all.shape[1]
    cmap = lambda b, h: (layer * nb + b, 0, h)
    nmap = lambda b, h: (b, h)
    return pl.pallas_call(
        functools.partial(_attn_sample_kernel, lam_init=lam_init),
        grid=(nb, nh),
        in_specs=[pl.BlockSpec((None, 4, w // 2), lambda b, h: (layer, 0, 0)),
                  pl.BlockSpec((None, 1, dv), lambda b, h: (layer, 0, 0)),
                  pl.BlockSpec((seq, w), nmap),
                  pl.BlockSpec((None, past, w), cmap),
                  pl.BlockSpec((None, past, dv), cmap),
                  pl.BlockSpec((seq, w), nmap),
                  pl.BlockSpec((seq, dv), nmap)],
        out_specs=pl.BlockSpec((seq, dv), nmap),
        out_shape=jax.ShapeDtypeStruct((m, nh * dv), BF16),
        compiler_params=_params("parallel", "parallel"),
        name="diff_attn_sample",
    )(lam_all, sub_all, q, kc_all, vc_all, kn, vn)


def _rope_tables(pos, hd):
    rot = hd // 4
    half = rot // 2
    inv = ROPE_THETA ** (-jnp.arange(0, rot, 2, dtype=F32) / rot)
    ang = pos.astype(F32)[:, None] * inv[None, :]
    cos, sin = jnp.cos(ang), jnp.sin(ang)
    n = pos.shape[0]
    ones = jnp.ones((n, hd - rot), F32)
    zeros_h = jnp.zeros((n, half), F32)
    zeros_r = jnp.zeros((n, hd - rot), F32)
    tab_cos = jnp.concatenate([cos, cos, ones], axis=1)
    tab_sa = jnp.concatenate([zeros_h, sin, zeros_r], axis=1)
    tab_sb = jnp.concatenate([-sin, zeros_h, zeros_r], axis=1)
    return tab_cos, tab_sa, tab_sb


def kernel(x_prompt, x_sample, cache_k, cache_v, state_gla, norm_mix, w_in, w_gate, b_gate,
           gla_gain, diff_lambda, diff_subln, w_out, norm_mlp, w_up, w_down, norm_final):
    depth, d_model, _ = w_in.shape
    bp, sp, _ = x_prompt.shape
    bs, ss, _ = x_sample.shape
    _, _, past, h_d, _, hd_d = cache_k.shape
    _, _, h_g, dk_g, dv_g = state_gla.shape
    rank = w_gate.shape[1]
    dv_d = cache_v.shape[-1]
    w_gqk = 2 * h_g * dk_g
    w_gvg = 2 * h_g * dv_g
    w_dq = h_d * 2 * hd_d
    w_dv = h_d * dv_d
    assert hd_d == LANES and hd_d // 8 == 16 and rank <= LANES

    o_ga = w_gqk + w_gvg
    w_main = jnp.concatenate([w_in[:, :, :o_ga], w_in[:, :, o_ga + rank:]], axis=2).astype(BF16)
    w_ga = jnp.pad(w_in[:, :, o_ga:o_ga + rank], ((0, 0), (0, 0), (0, LANES - rank))).astype(BF16)
    w_gate_p = jnp.pad(w_gate, ((0, 0), (0, LANES - rank), (0, 0)))
    w_out_b = w_out.astype(BF16)
    w_up_b = w_up.astype(BF16)
    w_down_b = w_down.astype(BF16)
    norm_mix3 = norm_mix[:, None, :]
    norm_mlp3 = norm_mlp[:, None, :]
    b_gate3 = b_gate[:, None, :]
    gain3 = gla_gain[:, None, :]
    subln3 = diff_subln[:, None, :]
    kc_all = cache_k.reshape(depth * bs, past, w_dq)
    vc_all = cache_v.reshape(depth * bs, past, w_dv)
    zero_state = jnp.zeros((1, bp, h_g, dk_g, dv_g), F32)
    tabs_p = _rope_tables(jnp.arange(sp), hd_d)
    tabs_s = _rope_tables(past + jnp.arange(ss), hd_d)

    c_dq = w_gqk + w_gvg
    c_dk = c_dq + w_dq
    c_dv = c_dk + w_dq

    def layer_fn(x, l, nb, seq, tabs, s0_all, s0_layer, cached):
        lam_init = 0.8 - 0.6 * math.exp(-0.3 * l)
        xn = _rmsnorm(x, norm_mix3, l, BF16)
        (qk,) = _proj(xn, w_main, l, 0, w_gqk, [F32], name="proj_gqk")
        (vg,) = _proj(xn, w_main, l, w_gqk, w_gvg, [BF16], name="proj_gvg")
        (ga,) = _proj(xn, w_ga, l, 0, LANES, [F32], name="proj_ga")
        (qd,) = _proj(xn, w_main, l, c_dq, w_dq, [BF16], rope_tabs=tabs, rows_per_seq=seq,
                      scale=hd_d ** -0.5 * math.log2(math.e), name="proj_dq")
        kd32, kd16 = _proj(xn, w_main, l, c_dk, w_dq, [F32, BF16], rope_tabs=tabs,
                           rows_per_seq=seq, name="proj_dk")
        vd32, vd16 = _proj(xn, w_main, l, c_dv, w_dv, [F32, BF16], name="proj_dv")
        yg, s_new = _gla(qk, vg, ga, w_gate_p, b_gate3, gain3, s0_all, l, s0_layer, nb, seq, h_g)
        if cached:
            yd = _attn_sample(qd, kd16, vd16, kc_all, vc_all, diff_lambda, subln3, l, lam_init,
                              nb, seq, h_d)
        else:
            yd = _attn_prompt(qd, kd16, vd16, diff_lambda, subln3, l, lam_init, nb, seq, h_d)
        x = _outproj(yg, yd, w_out_b, l, x)
        hn = _rmsnorm(x, norm_mlp3, l, BF16)
        (h,) = _proj(hn, w_up_b, l, 0, w_up_b.shape[2], [BF16], act=True, name="mlp_up")
        x = _down(h, w_down_b, l, x)
        return (x, kd32.reshape(nb, seq, h_d, 2, hd_d), vd32.reshape(nb, seq, h_d, dv_d), s_new)

    y_p = x_prompt.reshape(bp * sp, d_model)
    y_s = x_sample.reshape(bs * ss, d_model)
    kp, vp, stp, kss, vss, sts = [], [], [], [], [], []
    for l in range(depth):
        y_p, k_new, v_new, s_new = layer_fn(y_p, l, bp, sp, tabs_p, zero_state, 0, False)
        kp.append(k_new); vp.append(v_new); stp.append(s_new)
        y_s, k_new, v_new, s_new = layer_fn(y_s, l, bs, ss, tabs_s, state_gla, l, True)
        kss.append(k_new); vss.append(v_new); sts.append(s_new)
    nf3 = norm_final[None, None, :]
    y_p = _rmsnorm(y_p, nf3, 0, F32).reshape(bp, sp, d_model)
    y_s = _rmsnorm(y_s, nf3, 0, F32).reshape(bs, ss, d_model)
    return (y_p, y_s, jnp.stack(kp), jnp.stack(vp), jnp.stack(stp),
            jnp.stack(kss), jnp.stack(vss), jnp.stack(sts))
```

```python
import functools
import math

import jax
import jax.numpy as jnp
from jax import lax
from jax.experimental import pallas as pl
from jax.experimental.pallas import tpu as pltpu

CHUNK = 64
ROPE_THETA = 500000.0
GATE_TAU = 16.0
EPS = 1e-5
LANES = 128
VMEM_LIMIT = 56 * 1024 * 1024

F32 = jnp.float32
BF16 = jnp.bfloat16
_NT = (((1,), (1,)), ((), ()))
_TN = (((0,), (0,)), ((), ()))


def _tile(dim, pref):
    if dim <= pref:
        return dim
    t = pref
    while dim % t:
        t //= 2
    return t


def _params(*sem):
    return pltpu.CompilerParams(dimension_semantics=sem, vmem_limit_bytes=VMEM_LIMIT)


def _rmsnorm_kernel(x_ref, g_ref, o_ref):
    x = x_ref[...]
    r = lax.rsqrt(jnp.mean(x * x, axis=-1, keepdims=True) + EPS)
    o_ref[...] = (x * r * g_ref[...]).astype(o_ref.dtype)


def _rmsnorm(x, g_all, layer, out_dtype):
    m, d = x.shape
    tm = _tile(m, 512)
    return pl.pallas_call(
        _rmsnorm_kernel,
        grid=(m // tm,),
        in_specs=[pl.BlockSpec((tm, d), lambda i: (i, 0)),
                  pl.BlockSpec((None, 1, d), lambda i: (layer, 0, 0))],
        out_specs=pl.BlockSpec((tm, d), lambda i: (i, 0)),
        out_shape=jax.ShapeDtypeStruct((m, d), out_dtype),
        compiler_params=_params("parallel"),
        name="rmsnorm",
    )(x, g_all)


def _emit_norm_operands(x_new, g_ref, xg_ref, ssq_ref, first):
    xg_ref[...] = (x_new * g_ref[...]).astype(xg_ref.dtype)
    part = jnp.broadcast_to(jnp.sum(x_new * x_new, axis=-1, keepdims=True), ssq_ref.shape)

    @pl.when(first)
    def _():
        ssq_ref[...] = part

    @pl.when(jnp.logical_not(first))
    def _():
        ssq_ref[...] += part


def _prenorm_kernel(x_ref, g_ref, xg_ref, ssq_ref):
    _emit_norm_operands(x_ref[...], g_ref, xg_ref, ssq_ref, pl.program_id(1) == 0)


def _prenorm(x, g_all, layer):
    m, d = x.shape
    tm = _tile(m, 1024)
    tn = _tile(d, 1024)
    return pl.pallas_call(
        _prenorm_kernel,
        grid=(m // tm, d // tn),
        in_specs=[pl.BlockSpec((tm, tn), lambda i, j: (i, j)),
                  pl.BlockSpec((None, 1, tn), lambda i, j: (layer, 0, j))],
        out_specs=[pl.BlockSpec((tm, tn), lambda i, j: (i, j)),
                   pl.BlockSpec((tm, LANES), lambda i, j: (i, 0))],
        out_shape=[jax.ShapeDtypeStruct((m, d), BF16),
                   jax.ShapeDtypeStruct((m, LANES), F32)],
        compiler_params=_params("parallel", "arbitrary"),
        name="prenorm",
    )(x, g_all)


def _proj_kernel(*refs, rope, scale, act, n_out):
    a_ref, w_ref, ssq_ref = refs[0], refs[1], refs[2]
    outs = refs[len(refs) - n_out:]
    z = jnp.dot(a_ref[...], w_ref[...], preferred_element_type=F32)
    tn = z.shape[1]
    reps = tn // LANES
    r = lax.rsqrt(ssq_ref[...] * (1.0 / a_ref.shape[1]) + EPS)
    z = z * jnp.tile(r, (1, reps))
    if rope:
        cos_ref, sa_ref, sb_ref = refs[3], refs[4], refs[5]
        cos = jnp.tile(cos_ref[...], (1, reps))
        sa = jnp.tile(sa_ref[...], (1, reps))
        sb = jnp.tile(sb_ref[...], (1, reps))
        z = z * cos + pltpu.roll(z, 16, 1) * sa + pltpu.roll(z, tn - 16, 1) * sb
    if scale != 1.0:
        z = z * scale
    if act:
        z = jnp.square(jnp.maximum(z, 0.0))
    for o in outs:
        o[...] = z.astype(o.dtype)


def _proj(a, ssq, w_all, layer, col0, ncols, out_dtypes, *, rope_tabs=None, rows_per_seq=None,
          scale=1.0, act=False, name="proj"):
    m, k = a.shape
    tm = _tile(m, 1024)
    tn = _tile(math.gcd(col0, ncols), 512 if rope_tabs is not None else 1024)
    jb = col0 // tn
    in_specs = [pl.BlockSpec((tm, k), lambda i, j: (i, 0)),
                pl.BlockSpec((None, k, tn), lambda i, j: (layer, 0, j + jb)),
                pl.BlockSpec((tm, LANES), lambda i, j: (i, 0))]
    args = [a, w_all, ssq]
    if rope_tabs is not None:
        assert rows_per_seq % tm == 0 or tm % rows_per_seq == 0
        if tm >= rows_per_seq:
            tabs = [jnp.tile(t, (tm // rows_per_seq, 1)) for t in rope_tabs]
            tmap = lambda i, j: (0, 0)
        else:
            tabs = list(rope_tabs)
            nblk = rows_per_seq // tm
            tmap = lambda i, j: (i % nblk, 0)
        in_specs += [pl.BlockSpec((tm, LANES), tmap)] * 3
        args += tabs
    n_out = len(out_dtypes)
    outs = pl.pallas_call(
        functools.partial(_proj_kernel, rope=rope_tabs is not None, scale=scale, act=act,
                          n_out=n_out),
        grid=(m // tm, ncols // tn),
        in_specs=in_specs,
        out_specs=[pl.BlockSpec((tm, tn), lambda i, j: (i, j))] * n_out,
        out_shape=[jax.ShapeDtypeStruct((m, ncols), dt) for dt in out_dtypes],
        compiler_params=_params("parallel", "parallel"),
        name=name,
    )(*args)
    return outs


def _outproj_kernel(yg_ref, yd_ref, w_ref, res_ref, g_ref, o_ref, xg_ref, ssq_ref):
    kg = yg_ref.shape[1]
    z = jnp.dot(yg_ref[...], w_ref[:kg, :], preferred_element_type=F32)
    z = z + jnp.dot(yd_ref[...], w_ref[kg:, :], preferred_element_type=F32)
    x_new = res_ref[...] + z
    o_ref[...] = x_new
    _emit_norm_operands(x_new, g_ref, xg_ref, ssq_ref, pl.program_id(1) == 0)


def _outproj(yg, yd, w_all, layer, res, g_all):
    m, kg = yg.shape
    kd = yd.shape[1]
    n = w_all.shape[2]
    tm = _tile(m, 1024)
    tn = _tile(n, 512)
    return pl.pallas_call(
        _outproj_kernel,
        grid=(m // tm, n // tn),
        in_specs=[pl.BlockSpec((tm, kg), lambda i, j: (i, 0)),
                  pl.BlockSpec((tm, kd), lambda i, j: (i, 0)),
                  pl.BlockSpec((None, kg + kd, tn), lambda i, j: (layer, 0, j)),
                  pl.BlockSpec((tm, tn), lambda i, j: (i, j)),
                  pl.BlockSpec((None, 1, tn), lambda i, j: (layer, 0, j))],
        out_specs=[pl.BlockSpec((tm, tn), lambda i, j: (i, j)),
                   pl.BlockSpec((tm, tn), lambda i, j: (i, j)),
                   pl.BlockSpec((tm, LANES), lambda i, j: (i, 0))],
        out_shape=[jax.ShapeDtypeStruct((m, n), F32),
                   jax.ShapeDtypeStruct((m, n), BF16),
                   jax.ShapeDtypeStruct((m, LANES), F32)],
        compiler_params=_params("parallel", "arbitrary"),
        name="outproj",
    )(yg, yd, w_all, res, g_all)


def _down_kernel(h_ref, w_ref, res_ref, g_ref, o_ref, xg_ref, ssq_ref):
    z = jnp.dot(h_ref[...], w_ref[...], preferred_element_type=F32)
    kk = pl.program_id(2)

    @pl.when(kk == 0)
    def _():
        o_ref[...] = res_ref[...] + z

    @pl.when(kk > 0)
    def _():
        o_ref[...] += z

    @pl.when(kk == pl.num_programs(2) - 1)
    def _():
        _emit_norm_operands(o_ref[...], g_ref, xg_ref, ssq_ref, pl.program_id(1) == 0)


def _down(h, w_all, layer, res, g_all, g_layer):
    m, k = h.shape
    n = w_all.shape[2]
    tm = _tile(m, 1024)
    tn = _tile(n, 1024)
    tk = _tile(k, 2048)
    return pl.pallas_call(
        _down_kernel,
        grid=(m // tm, n // tn, k // tk),
        in_specs=[pl.BlockSpec((tm, tk), lambda i, j, kk: (i, kk)),
                  pl.BlockSpec((None, tk, tn), lambda i, j, kk: (layer, kk, j)),
                  pl.BlockSpec((tm, tn), lambda i, j, kk: (i, j)),
                  pl.BlockSpec((None, 1, tn), lambda i, j, kk: (g_layer, 0, j))],
        out_specs=[pl.BlockSpec((tm, tn), lambda i, j, kk: (i, j)),
                   pl.BlockSpec((tm, tn), lambda i, j, kk: (i, j)),
                   pl.BlockSpec((tm, LANES), lambda i, j, kk: (i, 0))],
        out_shape=[jax.ShapeDtypeStruct((m, n), F32),
                   jax.ShapeDtypeStruct((m, n), BF16),
                   jax.ShapeDtypeStruct((m, LANES), F32)],
        compiler_params=_params("parallel", "arbitrary", "arbitrary"),
        name="mlp_down",
    )(h, w_all, res, g_all)


def _split2(x):
    hi = x.astype(BF16)
    lo = (x - hi.astype(F32)).astype(BF16)
    return hi, lo


def _gla_kernel(q_ref, k_ref, v_ref, gg_ref, ga_ref, wg_ref, bg_ref, gain_ref, s0_ref,
                y_ref, sout_ref, st_ref, la_ref, qe_ref, kd_ref, oi_ref, ebl_ref,
                *, chunk, q_scale):
    t = pl.program_id(2)
    nchunk = q_ref.shape[0] // chunk
    unroll = 2 if nchunk % 2 == 0 else 1

    @pl.when(t == 0)
    def _():
        st_ref[...] = s0_ref[...].T

    ga_hi, ga_lo = _split2(ga_ref[...])
    w_hi, w_lo = _split2(wg_ref[...])
    pre = (jnp.dot(ga_hi, w_hi, preferred_element_type=F32)
           + jnp.dot(ga_lo, w_hi, preferred_element_type=F32)
           + jnp.dot(ga_hi, w_lo, preferred_element_type=F32)) + bg_ref[...]
    la_ref[...] = jax.nn.log_sigmoid(pre) / GATE_TAU

    row = lax.broadcasted_iota(jnp.int32, (chunk, chunk), 0)
    col = lax.broadcasted_iota(jnp.int32, (chunk, chunk), 1)
    tri = row >= col
    tri_b = tri.astype(BF16)

    group = 8 if nchunk % 8 == 0 else 1
    dk = q_ref.shape[1]

    def local(gi, carry):
        rows = pl.ds(pl.multiple_of(gi * (group * chunk), group * chunk), group * chunk)
        la_hi, la_lo = _split2(la_ref[rows, :])
        cs = [slice(g * chunk, (g + 1) * chunk) for g in range(group)]
        b_parts = [jnp.dot(tri_b, la_hi[c], preferred_element_type=F32)
                   + jnp.dot(tri_b, la_lo[c], preferred_element_type=F32)
                   for c in cs]
        bl_rows = [bp[chunk - 1:chunk, :] for bp in b_parts]
        b = jnp.concatenate(b_parts, axis=0)
        bl = jnp.concatenate([jnp.broadcast_to(r, (chunk, dk)) for r in bl_rows], axis=0)
        k = k_ref[rows, :]
        v = v_ref[rows, :]
        qe = (q_ref[rows, :] * q_scale * jnp.exp(b)).astype(BF16)
        ke = (k * jnp.exp(-b)).astype(BF16)
        oi_parts = []
        for c in cs:
            a = lax.dot_general(qe[c], ke[c], _NT, preferred_element_type=F32)
            a = jnp.where(tri, a, 0.0)
            oi_parts.append(jnp.dot(a.astype(BF16), v[c], preferred_element_type=F32))
        oi_ref[rows, :] = jnp.concatenate(oi_parts, axis=0)
        qe_ref[rows, :] = qe
        kd_ref[rows, :] = (k * jnp.exp(bl - b)).astype(BF16)
        ebl_ref[pl.ds(pl.multiple_of(gi * group, group), group), :] = jnp.exp(
            jnp.concatenate(bl_rows, axis=0))
        return carry

    lax.fori_loop(0, nchunk // group, local, 0)

    def serial(c, carry):
        rows = pl.ds(pl.multiple_of(c * chunk, chunk), chunk)
        st = st_ref[...]
        o = lax.dot_general(qe_ref[rows, :], st.astype(BF16), _NT, preferred_element_type=F32)
        o = o + oi_ref[rows, :]
        st_ref[...] = st * ebl_ref[pl.ds(c, 1), :] + lax.dot_general(
            v_ref[rows, :], kd_ref[rows, :], _TN, preferred_element_type=F32)
        r = lax.rsqrt(jnp.mean(o * o, axis=-1, keepdims=True) + EPS)
        gg = gg_ref[rows, :].astype(F32)
        y = (o * r * gain_ref[...]) * (gg * jax.nn.sigmoid(gg))
        y_ref[rows, :] = y.astype(y_ref.dtype)
        return carry

    lax.fori_loop(0, nchunk, serial, 0, unroll=unroll)

    @pl.when(t == pl.num_programs(2) - 1)
    def _():
        sout_ref[...] = st_ref[...].T


def _gla(qk, vg, ga, wg_all, bg_all, gain_all, s0_all, layer, s0_layer, nb, seq, nh):
    m = qk.shape[0]
    dk = qk.shape[1] // (2 * nh)
    dv = vg.shape[1] // (2 * nh)
    chunk = min(seq, CHUNK)
    tt = _tile(seq, 1024)
    nt = seq // tt
    rmap = lambda b, h, t: (b * nt + t, h)
    rmap2 = lambda b, h, t: (b * nt + t, h + nh)
    y, s_out = pl.pallas_call(
        functools.partial(_gla_kernel, chunk=chunk, q_scale=dk ** -0.5),
        grid=(nb, nh, nt),
        in_specs=[pl.BlockSpec((tt, dk), rmap),
                  pl.BlockSpec((tt, dk), rmap2),
                  pl.BlockSpec((tt, dv), rmap),
                  pl.BlockSpec((tt, dv), rmap2),
                  pl.BlockSpec((tt, LANES), lambda b, h, t: (b * nt + t, 0)),
                  pl.BlockSpec((None, LANES, dk), lambda b, h, t: (layer, 0, h)),
                  pl.BlockSpec((None, 1, dk), lambda b, h, t: (layer, 0, h)),
                  pl.BlockSpec((None, 1, dv), lambda b, h, t: (layer, 0, 0)),
                  pl.BlockSpec((None, None, None, dk, dv),
                               lambda b, h, t: (s0_layer, b, h, 0, 0))],
        out_specs=[pl.BlockSpec((tt, dv), rmap),
                   pl.BlockSpec((None, None, dk, dv), lambda b, h, t: (b, h, 0, 0))],
        out_shape=[jax.ShapeDtypeStruct((m, nh * dv), BF16),
                   jax.ShapeDtypeStruct((nb, nh, dk, dv), F32)],
        scratch_shapes=[pltpu.VMEM((dv, dk), F32),
                        pltpu.VMEM((tt, dk), F32),
                        pltpu.VMEM((tt, dk), BF16),
                        pltpu.VMEM((tt, dk), BF16),
                        pltpu.VMEM((tt, dv), F32),
                        pltpu.VMEM((max(tt // chunk, 8), dk), F32)],
        compiler_params=_params("parallel", "parallel", "arbitrary"),
        name="gla",
    )(qk, qk, vg, vg, ga, wg_all, bg_all, gain_all, s0_all)
    return y, s_out


def _softmax_av(score_parts, value_parts):
    m = functools.reduce(jnp.maximum,
                         [jnp.max(s, axis=-1, keepdims=True) for s in score_parts])
    ps = [jnp.exp2(s - m) for s in score_parts]
    l = functools.reduce(jnp.add, [jnp.sum(p, axis=-1, keepdims=True) for p in ps])
    acc = functools.reduce(jnp.add, [jnp.dot(p.astype(BF16), v, preferred_element_type=F32)
                                     for p, v in zip(ps, value_parts)])
    return acc * (1.0 / l)


def _diff_combine(heads, lam_ref, sub_ref, lam_init):
    lp = lam_ref[...]
    lam = (jnp.exp(jnp.sum(lp[0:1] * lp[1:2], axis=-1, keepdims=True))
           - jnp.exp(jnp.sum(lp[2:3] * lp[3:4], axis=-1, keepdims=True)) + lam_init)
    o = heads[0] - lam * heads[1]
    r = lax.rsqrt(jnp.mean(o * o, axis=-1, keepdims=True) + EPS)
    return (o * r * sub_ref[...]) * (1.0 - lam_init)


def _visibility_bias(qpos, kpos):
    return jnp.where((kpos // CHUNK) <= (qpos // CHUNK), 0.0, -jnp.inf).astype(F32)


def _attn_prompt_kernel(lam_ref, sub_ref, q_ref, k_ref, v_ref, o_ref, *, lam_init, tq):
    hd = q_ref.shape[1] // 2
    diag_bias = _visibility_bias(lax.broadcasted_iota(jnp.int32, (tq, tq), 0),
                                 lax.broadcasted_iota(jnp.int32, (tq, tq), 1))
    for qi in range(q_ref.shape[0] // tq):
        q0, q1 = qi * tq, (qi + 1) * tq
        heads = []
        for c in range(2):
            cols = slice(c * hd, (c + 1) * hd)
            qc = q_ref[q0:q1, cols]
            scores = [lax.dot_general(qc, k_ref[q0:q1, cols], _NT,
                                      preferred_element_type=F32) + diag_bias]
            values = [v_ref[q0:q1, :]]
            if qi:
                scores.append(lax.dot_general(qc, k_ref[0:q0, cols], _NT,
                                              preferred_element_type=F32))
                values.append(v_ref[0:q0, :])
            heads.append(_softmax_av(scores, values))
        o_ref[q0:q1, :] = _diff_combine(heads, lam_ref, sub_ref, lam_init).astype(o_ref.dtype)


def _attn_prompt(q, k, v, lam_all, sub_all, layer, lam_init, nb, seq, nh):
    m = q.shape[0]
    w = q.shape[1] // nh
    dv = v.shape[1] // nh
    tq = _tile(seq, 256)
    assert tq % CHUNK == 0
    bmap = lambda b, h: (b, h)
    return pl.pallas_call(
        functools.partial(_attn_prompt_kernel, lam_init=lam_init, tq=tq),
        grid=(nb, nh),
        in_specs=[pl.BlockSpec((None, 4, w // 2), lambda b, h: (layer, 0, 0)),
                  pl.BlockSpec((None, 1, dv), lambda b, h: (layer, 0, 0)),
                  pl.BlockSpec((seq, w), bmap),
                  pl.BlockSpec((seq, w), bmap),
                  pl.BlockSpec((seq, dv), bmap)],
        out_specs=pl.BlockSpec((seq, dv), bmap),
        out_shape=jax.ShapeDtypeStruct((m, nh * dv), BF16),
        compiler_params=_params("parallel", "parallel"),
        name="diff_attn_prompt",
    )(lam_all, sub_all, q, k, v)


def _attn_sample_kernel(lam_ref, sub_ref, q_ref, kc_ref, vc_ref, kn_ref, vn_ref, o_ref,
                        *, lam_init):
    tq = q_ref.shape[0]
    past = kc_ref.shape[0]
    tn = kn_ref.shape[0]
    hd = q_ref.shape[1] // 2
    qpos = past + lax.broadcasted_iota(jnp.int32, (tq, 1), 0)
    bias_c = _visibility_bias(qpos, lax.broadcasted_iota(jnp.int32, (tq, past), 1))
    bias_n = _visibility_bias(qpos, past + lax.broadcasted_iota(jnp.int32, (tq, tn), 1))
    values = [vc_ref[...].astype(BF16), vn_ref[...]]
    heads = []
    for c in range(2):
        cols = slice(c * hd, (c + 1) * hd)
        qc = q_ref[:, cols]
        scores = [lax.dot_general(qc, kc_ref[:, cols].astype(BF16), _NT,
                                  preferred_element_type=F32) + bias_c,
                  lax.dot_general(qc, kn_ref[:, cols], _NT,
                                  preferred_element_type=F32) + bias_n]
        heads.append(_softmax_av(scores, values))
    o_ref[...] = _diff_combine(heads, lam_ref, sub_ref, lam_init).astype(o_ref.dtype)


def _attn_sample(q, kn, vn, kc_all, vc_all, lam_all, sub_all, layer, lam_init, nb, seq, nh):
    m = q.shape[0]
    w = q.shape[1] // nh
    dv = vn.shape[1] // nh
    past = kc_all.shape[1]
    cmap = lambda b, h: (layer * nb + b, 0, h)
    nmap = lambda b, h: (b, h)
    return pl.pallas_call(
        functools.partial(_attn_sample_kernel, lam_init=lam_init),
        grid=(nb, nh),
        in_specs=[pl.BlockSpec((None, 4, w // 2), lambda b, h: (layer, 0, 0)),
                  pl.BlockSpec((None, 1, dv), lambda b, h: (layer, 0, 0)),
                  pl.BlockSpec((seq, w), nmap),
                  pl.BlockSpec((None, past, w), cmap),
                  pl.BlockSpec((None, past, dv), cmap),
                  pl.BlockSpec((seq, w), nmap),
                  pl.BlockSpec((seq, dv), nmap)],
        out_specs=pl.BlockSpec((seq, dv), nmap),
        out_shape=jax.ShapeDtypeStruct((m, nh * dv), BF16),
        compiler_params=_params("parallel", "parallel"),
        name="diff_attn_sample",
    )(lam_all, sub_all, q, kc_all, vc_all, kn, vn)


def _rope_tables(pos, hd):
    rot = hd // 4
    half = rot // 2
    inv = ROPE_THETA ** (-jnp.arange(0, rot, 2, dtype=F32) / rot)
    ang = pos.astype(F32)[:, None] * inv[None, :]
    cos, sin = jnp.cos(ang), jnp.sin(ang)
    n = pos.shape[0]
    ones = jnp.ones((n, hd - rot), F32)
    zeros_h = jnp.zeros((n, half), F32)
    zeros_r = jnp.zeros((n, hd - rot), F32)
    tab_cos = jnp.concatenate([cos, cos, ones], axis=1)
    tab_sa = jnp.concatenate([zeros_h, sin, zeros_r], axis=1)
    tab_sb = jnp.concatenate([-sin, zeros_h, zeros_r], axis=1)
    return tab_cos, tab_sa, tab_sb


def kernel(x_prompt, x_sample, cache_k, cache_v, state_gla, norm_mix, w_in, w_gate, b_gate,
           gla_gain, diff_lambda, diff_subln, w_out, norm_mlp, w_up, w_down, norm_final):
    depth, d_model, _ = w_in.shape
    bp, sp, _ = x_prompt.shape
    bs, ss, _ = x_sample.shape
    _, _, past, h_d, _, hd_d = cache_k.shape
    _, _, h_g, dk_g, dv_g = state_gla.shape
    rank = w_gate.shape[1]
    dv_d = cache_v.shape[-1]
    w_gqk = 2 * h_g * dk_g
    w_gvg = 2 * h_g * dv_g
    w_dq = h_d * 2 * hd_d
    w_dv = h_d * dv_d
    assert hd_d == LANES and hd_d // 8 == 16 and rank <= LANES

    o_ga = w_gqk + w_gvg
    w_gla = w_in[:, :, :o_ga].astype(BF16)
    w_diff = w_in[:, :, o_ga + rank:].astype(BF16)
    w_ga = jnp.pad(w_in[:, :, o_ga:o_ga + rank], ((0, 0), (0, 0), (0, LANES - rank))).astype(BF16)
    w_gate_p = jnp.pad(w_gate, ((0, 0), (0, LANES - rank), (0, 0)))
    w_out_b = w_out.astype(BF16)
    w_up_b = w_up.astype(BF16)
    w_down_b = w_down.astype(BF16)
    norm_mix3 = norm_mix[:, None, :]
    norm_mlp3 = norm_mlp[:, None, :]
    b_gate3 = b_gate[:, None, :]
    gain3 = gla_gain[:, None, :]
    subln3 = diff_subln[:, None, :]
    kc_all = cache_k.reshape(depth * bs, past, w_dq)
    vc_all = cache_v.reshape(depth * bs, past, w_dv)
    zero_state = jnp.zeros((1, bp, h_g, dk_g, dv_g), F32)
    tabs_p = _rope_tables(jnp.arange(sp), hd_d)
    tabs_s = _rope_tables(past + jnp.arange(ss), hd_d)

    def layer_fn(stream, l, nb, seq, tabs, s0_all, s0_layer, cached):
        x, xg, ssq = stream
        lam_init = 0.8 - 0.6 * math.exp(-0.3 * l)
        (qk,) = _proj(xg, ssq, w_gla, l, 0, w_gqk, [F32], name="proj_gqk")
        (vg,) = _proj(xg, ssq, w_gla, l, w_gqk, w_gvg, [BF16], name="proj_gvg")
        (ga,) = _proj(xg, ssq, w_ga, l, 0, LANES, [F32], name="proj_ga")
        (qd,) = _proj(xg, ssq, w_diff, l, 0, w_dq, [BF16], rope_tabs=tabs, rows_per_seq=seq,
                      scale=hd_d ** -0.5 * math.log2(math.e), name="proj_dq")
        kd32, kd16 = _proj(xg, ssq, w_diff, l, w_dq, w_dq, [F32, BF16], rope_tabs=tabs,
                           rows_per_seq=seq, name="proj_dk")
        vd32, vd16 = _proj(xg, ssq, w_diff, l, 2 * w_dq, w_dv, [F32, BF16], name="proj_dv")
        yg, s_new = _gla(qk, vg, ga, w_gate_p, b_gate3, gain3, s0_all, l, s0_layer, nb, seq, h_g)
        if cached:
            yd = _attn_sample(qd, kd16, vd16, kc_all, vc_all, diff_lambda, subln3, l, lam_init,
                              nb, seq, h_d)
        else:
            yd = _attn_prompt(qd, kd16, vd16, diff_lambda, subln3, l, lam_init, nb, seq, h_d)
        x, hg, hssq = _outproj(yg, yd, w_out_b, l, x, norm_mlp3)
        (h,) = _proj(hg, hssq, w_up_b, l, 0, w_up_b.shape[2], [BF16], act=True, name="mlp_up")
        stream = _down(h, w_down_b, l, x, norm_mix3, min(l + 1, depth - 1))
        return (stream, kd32.reshape(nb, seq, h_d, 2, hd_d), vd32.reshape(nb, seq, h_d, dv_d),
                s_new)

    def start(x):
        return (x,) + tuple(_prenorm(x, norm_mix3, 0))

    st_p = start(x_prompt.reshape(bp * sp, d_model))
    st_s = start(x_sample.reshape(bs * ss, d_model))
    kp, vp, stp, kss, vss, sts = [], [], [], [], [], []
    for l in range(depth):
        st_p, k_new, v_new, s_new = layer_fn(st_p, l, bp, sp, tabs_p, zero_state, 0, False)
        kp.append(k_new); vp.append(v_new); stp.append(s_new)
        st_s, k_new, v_new, s_new = layer_fn(st_s, l, bs, ss, tabs_s, state_gla, l, True)
        kss.append(k_new); vss.append(v_new); sts.append(s_new)
    nf3 = norm_final[None, None, :]
    y_p = _rmsnorm(st_p[0], nf3, 0, F32).reshape(bp, sp, d_model)
    y_s = _rmsnorm(st_s[0], nf3, 0, F32).reshape(bs, ss, d_model)
    return (y_p, y_s, jnp.stack(kp), jnp.stack(vp), jnp.stack(stp),
            jnp.stack(kss), jnp.stack(vss), jnp.stack(sts))
```

```python
import functools
import math

import jax
import jax.numpy as jnp
from jax import lax
from jax.experimental import pallas as pl
from jax.experimental.pallas import tpu as pltpu

CHUNK = 64
ROPE_THETA = 500000.0
GATE_TAU = 16.0
EPS = 1e-5
LANES = 128
SUBLANES = 8
VMEM_LIMIT = 56 * 1024 * 1024

F32 = jnp.float32
BF16 = jnp.bfloat16
_NT = (((1,), (1,)), ((), ()))
_TN = (((0,), (0,)), ((), ()))


def _tile(dim, pref):
    if dim <= pref:
        return dim
    t = pref
    while dim % t:
        t //= 2
    return t


def _params(*sem):
    return pltpu.CompilerParams(dimension_semantics=sem, vmem_limit_bytes=VMEM_LIMIT)


def _rmsnorm_kernel(x_ref, g_ref, o_ref):
    x = x_ref[...]
    r = lax.rsqrt(jnp.mean(x * x, axis=-1, keepdims=True) + EPS)
    o_ref[...] = (x * r * g_ref[...]).astype(o_ref.dtype)


def _rmsnorm(x, g_all, layer, out_dtype):
    m, d = x.shape
    tm = _tile(m, 512)
    return pl.pallas_call(
        _rmsnorm_kernel,
        grid=(m // tm,),
        in_specs=[pl.BlockSpec((tm, d), lambda i: (i, 0)),
                  pl.BlockSpec((None, 1, d), lambda i: (layer, 0, 0))],
        out_specs=pl.BlockSpec((tm, d), lambda i: (i, 0)),
        out_shape=jax.ShapeDtypeStruct((m, d), out_dtype),
        compiler_params=_params("parallel"),
        name="rmsnorm",
    )(x, g_all)


def _emit_norm_operands(x_new, g_ref, xg_ref, ssq_ref, first):
    xg_ref[...] = (x_new * g_ref[...]).astype(xg_ref.dtype)
    part = jnp.broadcast_to(jnp.sum(x_new * x_new, axis=-1, keepdims=True), ssq_ref.shape)

    @pl.when(first)
    def _():
        ssq_ref[...] = part

    @pl.when(jnp.logical_not(first))
    def _():
        ssq_ref[...] += part


def _prenorm_kernel(x_ref, g_ref, xg_ref, ssq_ref):
    _emit_norm_operands(x_ref[...], g_ref, xg_ref, ssq_ref, pl.program_id(1) == 0)


def _prenorm(x, g_all, layer):
    m, d = x.shape
    tm = _tile(m, 1024)
    tn = _tile(d, 1024)
    return pl.pallas_call(
        _prenorm_kernel,
        grid=(m // tm, d // tn),
        in_specs=[pl.BlockSpec((tm, tn), lambda i, j: (i, j)),
                  pl.BlockSpec((None, 1, tn), lambda i, j: (layer, 0, j))],
        out_specs=[pl.BlockSpec((tm, tn), lambda i, j: (i, j)),
                   pl.BlockSpec((tm, LANES), lambda i, j: (i, 0))],
        out_shape=[jax.ShapeDtypeStruct((m, d), BF16),
                   jax.ShapeDtypeStruct((m, LANES), F32)],
        compiler_params=_params("parallel", "arbitrary"),
        name="prenorm",
    )(x, g_all)


def _proj_kernel(*refs, rope, scale, act, split, side, n_out):
    a_ref, w_ref, ssq_ref = refs[0], refs[1], refs[2]
    n_in = 4 if side else 3
    outs = refs[len(refs) - n_out:]
    z = jnp.dot(a_ref[...], w_ref[...], preferred_element_type=F32)
    tn = z.shape[1]
    reps = tn // LANES
    r = lax.rsqrt(ssq_ref[...] * (1.0 / a_ref.shape[1]) + EPS)
    z = z * jnp.tile(r, (1, reps))
    if side:
        side_ref, outs = outs[-1], outs[:-1]

        @pl.when(pl.program_id(1) == 0)
        def _():
            zs = jnp.dot(a_ref[...], refs[3][...], preferred_element_type=F32)
            side_ref[...] = (zs * r).astype(side_ref.dtype)

    if rope:
        cos_ref, sa_ref, sb_ref = refs[n_in:n_in + 3]
        cos = jnp.tile(cos_ref[...], (1, reps))
        sa = jnp.tile(sa_ref[...], (1, reps))
        sb = jnp.tile(sb_ref[...], (1, reps))
        z = z * cos + pltpu.roll(z, 16, 1) * sa + pltpu.roll(z, tn - 16, 1) * sb
    if scale != 1.0:
        z = z * scale
    if act:
        z = jnp.square(jnp.maximum(z, 0.0))
    if split:
        o3, outs = outs[0], outs[1:]
        for g in range(o3.shape[1]):
            o3[:, g, :] = z[:, g * split:(g + 1) * split].astype(o3.dtype)
    for o in outs:
        o[...] = z.astype(o.dtype)


def _proj(a, ssq, w_all, layer, col0, ncols, out_dtypes, *, rope_tabs=None, rows_per_seq=None,
          scale=1.0, act=False, split=0, side_w=None, tm_pref=1024, name="proj"):
    m, k = a.shape
    tm = _tile(m, tm_pref)
    if split:
        tn = SUBLANES * split
        assert col0 % tn == 0 and ncols % tn == 0
    else:
        tn = _tile(math.gcd(col0, ncols), 512 if rope_tabs is not None else 1024)
    jb = col0 // tn
    in_specs = [pl.BlockSpec((tm, k), lambda i, j: (i, 0)),
                pl.BlockSpec((None, k, tn), lambda i, j: (layer, 0, j + jb)),
                pl.BlockSpec((tm, LANES), lambda i, j: (i, 0))]
    args = [a, w_all, ssq]
    if side_w is not None:
        in_specs.append(pl.BlockSpec((None, k, LANES), lambda i, j: (layer, 0, 0)))
        args.append(side_w)
    if rope_tabs is not None:
        assert rows_per_seq % tm == 0 or tm % rows_per_seq == 0
        if tm >= rows_per_seq:
            tabs = [jnp.tile(t, (tm // rows_per_seq, 1)) for t in rope_tabs]
            tmap = lambda i, j: (0, 0)
        else:
            tabs = list(rope_tabs)
            nblk = rows_per_seq // tm
            tmap = lambda i, j: (i % nblk, 0)
        in_specs += [pl.BlockSpec((tm, LANES), tmap)] * 3
        args += tabs
    n_out = len(out_dtypes)
    out_specs = [pl.BlockSpec((tm, tn), lambda i, j: (i, j))] * n_out
    out_shape = [jax.ShapeDtypeStruct((m, ncols), dt) for dt in out_dtypes]
    if split:
        out_specs[0] = pl.BlockSpec((tm, SUBLANES, split), lambda i, j: (i, j, 0))
        out_shape[0] = jax.ShapeDtypeStruct((m, ncols // split, split), out_dtypes[0])
    if side_w is not None:
        out_specs.append(pl.BlockSpec((tm, LANES), lambda i, j: (i, 0)))
        out_shape.append(jax.ShapeDtypeStruct((m, LANES), F32))
    outs = pl.pallas_call(
        functools.partial(_proj_kernel, rope=rope_tabs is not None, scale=scale, act=act,
                          split=split, side=side_w is not None, n_out=len(out_specs)),
        grid=(m // tm, ncols // tn),
        in_specs=in_specs,
        out_specs=out_specs,
        out_shape=out_shape,
        compiler_params=_params("parallel", "arbitrary" if side_w is not None else "parallel"),
        name=name,
    )(*args)
    return outs


def _outproj_kernel(yg_ref, yd_ref, w_ref, res_ref, g_ref, o_ref, xg_ref, ssq_ref):
    kg = yg_ref.shape[1]
    z = jnp.dot(yg_ref[...], w_ref[:kg, :], preferred_element_type=F32)
    z = z + jnp.dot(yd_ref[...], w_ref[kg:, :], preferred_element_type=F32)
    x_new = res_ref[...] + z
    o_ref[...] = x_new
    _emit_norm_operands(x_new, g_ref, xg_ref, ssq_ref, pl.program_id(1) == 0)


def _outproj(yg, yd, w_all, layer, res, g_all):
    m, kg = yg.shape
    kd = yd.shape[1]
    n = w_all.shape[2]
    tm = _tile(m, 1024)
    tn = _tile(n, 512)
    return pl.pallas_call(
        _outproj_kernel,
        grid=(m // tm, n // tn),
        in_specs=[pl.BlockSpec((tm, kg), lambda i, j: (i, 0)),
                  pl.BlockSpec((tm, kd), lambda i, j: (i, 0)),
                  pl.BlockSpec((None, kg + kd, tn), lambda i, j: (layer, 0, j)),
                  pl.BlockSpec((tm, tn), lambda i, j: (i, j)),
                  pl.BlockSpec((None, 1, tn), lambda i, j: (layer, 0, j))],
        out_specs=[pl.BlockSpec((tm, tn), lambda i, j: (i, j)),
                   pl.BlockSpec((tm, tn), lambda i, j: (i, j)),
                   pl.BlockSpec((tm, LANES), lambda i, j: (i, 0))],
        out_shape=[jax.ShapeDtypeStruct((m, n), F32),
                   jax.ShapeDtypeStruct((m, n), BF16),
                   jax.ShapeDtypeStruct((m, LANES), F32)],
        compiler_params=_params("parallel", "arbitrary"),
        name="outproj",
    )(yg, yd, w_all, res, g_all)


def _down_kernel(h_ref, w_ref, res_ref, g_ref, o_ref, xg_ref, ssq_ref):
    z = jnp.dot(h_ref[...], w_ref[...], preferred_element_type=F32)
    kk = pl.program_id(2)

    @pl.when(kk == 0)
    def _():
        o_ref[...] = res_ref[...] + z

    @pl.when(kk > 0)
    def _():
        o_ref[...] += z

    @pl.when(kk == pl.num_programs(2) - 1)
    def _():
        _emit_norm_operands(o_ref[...], g_ref, xg_ref, ssq_ref, pl.program_id(1) == 0)


def _down(h, w_all, layer, res, g_all, g_layer):
    m, k = h.shape
    n = w_all.shape[2]
    tm = _tile(m, 1024)
    tn = _tile(n, 1024)
    tk = _tile(k, 2048)
    return pl.pallas_call(
        _down_kernel,
        grid=(m // tm, n // tn, k // tk),
        in_specs=[pl.BlockSpec((tm, tk), lambda i, j, kk: (i, kk)),
                  pl.BlockSpec((None, tk, tn), lambda i, j, kk: (layer, kk, j)),
                  pl.BlockSpec((tm, tn), lambda i, j, kk: (i, j)),
                  pl.BlockSpec((None, 1, tn), lambda i, j, kk: (g_layer, 0, j))],
        out_specs=[pl.BlockSpec((tm, tn), lambda i, j, kk: (i, j)),
                   pl.BlockSpec((tm, tn), lambda i, j, kk: (i, j)),
                   pl.BlockSpec((tm, LANES), lambda i, j, kk: (i, 0))],
        out_shape=[jax.ShapeDtypeStruct((m, n), F32),
                   jax.ShapeDtypeStruct((m, n), BF16),
                   jax.ShapeDtypeStruct((m, LANES), F32)],
        compiler_params=_params("parallel", "arbitrary", "arbitrary"),
        name="mlp_down",
    )(h, w_all, res, g_all)


def _split2(x):
    hi = x.astype(BF16)
    lo = (x - hi.astype(F32)).astype(BF16)
    return hi, lo


def _gla_kernel(q_ref, k_ref, v_ref, gg_ref, ga_ref, wg_ref, bg_ref, gain_ref, s0_ref,
                y_ref, sout_ref, st_ref, la_ref, qe_ref, kd_ref, oi_ref, ebl_ref,
                *, chunk, q_scale):
    t = pl.program_id(2)
    nchunk = q_ref.shape[0] // chunk
    unroll = 2 if nchunk % 2 == 0 else 1

    @pl.when(t == 0)
    def _():
        st_ref[...] = s0_ref[...].T

    ga_hi, ga_lo = _split2(ga_ref[...])
    w_hi, w_lo = _split2(wg_ref[...])
    pre = (jnp.dot(ga_hi, w_hi, preferred_element_type=F32)
           + jnp.dot(ga_lo, w_hi, preferred_element_type=F32)
           + jnp.dot(ga_hi, w_lo, preferred_element_type=F32)) + bg_ref[...]
    la_ref[...] = jax.nn.log_sigmoid(pre) / GATE_TAU

    row = lax.broadcasted_iota(jnp.int32, (chunk, chunk), 0)
    col = lax.broadcasted_iota(jnp.int32, (chunk, chunk), 1)
    tri = row >= col
    tri_b = tri.astype(BF16)

    group = 8 if nchunk % 8 == 0 else 1
    dk = q_ref.shape[1]

    def local(gi, carry):
        rows = pl.ds(pl.multiple_of(gi * (group * chunk), group * chunk), group * chunk)
        la_hi, la_lo = _split2(la_ref[rows, :])
        cs = [slice(g * chunk, (g + 1) * chunk) for g in range(group)]
        b_parts = [jnp.dot(tri_b, la_hi[c], preferred_element_type=F32)
                   + jnp.dot(tri_b, la_lo[c], preferred_element_type=F32)
                   for c in cs]
        bl_rows = [bp[chunk - 1:chunk, :] for bp in b_parts]
        b = jnp.concatenate(b_parts, axis=0)
        bl = jnp.concatenate([jnp.broadcast_to(r, (chunk, dk)) for r in bl_rows], axis=0)
        k = k_ref[rows, :]
        v = v_ref[rows, :]
        qe = (q_ref[rows, :] * q_scale * jnp.exp(b)).astype(BF16)
        ke = (k * jnp.exp(-b)).astype(BF16)
        oi_parts = []
        for c in cs:
            a = lax.dot_general(qe[c], ke[c], _NT, preferred_element_type=F32)
            a = jnp.where(tri, a, 0.0)
            oi_parts.append(jnp.dot(a.astype(BF16), v[c], preferred_element_type=F32))
        oi_ref[rows, :] = jnp.concatenate(oi_parts, axis=0)
        qe_ref[rows, :] = qe
        kd_ref[rows, :] = (k * jnp.exp(bl - b)).astype(BF16)
        ebl_ref[pl.ds(pl.multiple_of(gi * group, group), group), :] = jnp.exp(
            jnp.concatenate(bl_rows, axis=0))
        return carry

    lax.fori_loop(0, nchunk // group, local, 0)

    def serial(c, carry):
        rows = pl.ds(pl.multiple_of(c * chunk, chunk), chunk)
        st = st_ref[...]
        o = lax.dot_general(qe_ref[rows, :], st.astype(BF16), _NT, preferred_element_type=F32)
        o = o + oi_ref[rows, :]
        st_ref[...] = st * ebl_ref[pl.ds(c, 1), :] + lax.dot_general(
            v_ref[rows, :], kd_ref[rows, :], _TN, preferred_element_type=F32)
        r = lax.rsqrt(jnp.mean(o * o, axis=-1, keepdims=True) + EPS)
        gg = gg_ref[rows, :].astype(F32)
        y = (o * r * gain_ref[...]) * (gg * jax.nn.sigmoid(gg))
        y_ref[rows, :] = y.astype(y_ref.dtype)
        return carry

    lax.fori_loop(0, nchunk, serial, 0, unroll=unroll)

    @pl.when(t == pl.num_programs(2) - 1)
    def _():
        sout_ref[...] = st_ref[...].T


def _gla(qk, vg, ga, wg_all, bg_all, gain_all, s0_all, layer, s0_layer, nb, seq, nh):
    m = qk.shape[0]
    dk = qk.shape[1] // (2 * nh)
    dv = vg.shape[1] // (2 * nh)
    chunk = min(seq, CHUNK)
    tt = _tile(seq, 1024)
    nt = seq // tt
    rmap = lambda b, h, t: (b * nt + t, h)
    rmap2 = lambda b, h, t: (b * nt + t, h + nh)
    y, s_out = pl.pallas_call(
        functools.partial(_gla_kernel, chunk=chunk, q_scale=dk ** -0.5),
        grid=(nb, nh, nt),
        in_specs=[pl.BlockSpec((tt, dk), rmap),
                  pl.BlockSpec((tt, dk), rmap2),
                  pl.BlockSpec((tt, dv), rmap),
                  pl.BlockSpec((tt, dv), rmap2),
                  pl.BlockSpec((tt, LANES), lambda b, h, t: (b * nt + t, 0)),
                  pl.BlockSpec((None, LANES, dk), lambda b, h, t: (layer, 0, h)),
                  pl.BlockSpec((None, 1, dk), lambda b, h, t: (layer, 0, h)),
                  pl.BlockSpec((None, 1, dv), lambda b, h, t: (layer, 0, 0)),
                  pl.BlockSpec((None, None, None, dk, dv),
                               lambda b, h, t: (s0_layer, b, h, 0, 0))],
        out_specs=[pl.BlockSpec((tt, dv), rmap),
                   pl.BlockSpec((None, None, dk, dv), lambda b, h, t: (b, h, 0, 0))],
        out_shape=[jax.ShapeDtypeStruct((m, nh * dv), BF16),
                   jax.ShapeDtypeStruct((nb, nh, dk, dv), F32)],
        scratch_shapes=[pltpu.VMEM((dv, dk), F32),
                        pltpu.VMEM((tt, dk), F32),
                        pltpu.VMEM((tt, dk), BF16),
                        pltpu.VMEM((tt, dk), BF16),
                        pltpu.VMEM((tt, dv), F32),
                        pltpu.VMEM((max(tt // chunk, 8), dk), F32)],
        compiler_params=_params("parallel", "parallel", "arbitrary"),
        name="gla",
    )(qk, qk, vg, vg, ga, wg_all, bg_all, gain_all, s0_all)
    return y, s_out


def _softmax_av(score_parts, value_parts):
    m = functools.reduce(jnp.maximum,
                         [jnp.max(s, axis=-1, keepdims=True) for s in score_parts])
    ps = [jnp.exp2(s - m) for s in score_parts]
    l = functools.reduce(jnp.add, [jnp.sum(p, axis=-1, keepdims=True) for p in ps])
    acc = functools.reduce(jnp.add, [jnp.dot(p.astype(BF16), v, preferred_element_type=F32)
                                     for p, v in zip(ps, value_parts)])
    return acc * (1.0 / l)


def _diff_combine(heads, lam_ref, sub_ref, lam_init):
    lp = lam_ref[...]
    lam = (jnp.exp(jnp.sum(lp[0:1] * lp[1:2], axis=-1, keepdims=True))
           - jnp.exp(jnp.sum(lp[2:3] * lp[3:4], axis=-1, keepdims=True)) + lam_init)
    o = heads[0] - lam * heads[1]
    r = lax.rsqrt(jnp.mean(o * o, axis=-1, keepdims=True) + EPS)
    return (o * r * sub_ref[...]) * (1.0 - lam_init)


def _visibility_bias(qpos, kpos):
    return jnp.where((kpos // CHUNK) <= (qpos // CHUNK), 0.0, -jnp.inf).astype(F32)


def _attn_prompt_kernel(lam_ref, sub_ref, q_ref, k_ref, v_ref, o_ref, *, lam_init, tq):
    hd = q_ref.shape[1] // 2
    diag_bias = _visibility_bias(lax.broadcasted_iota(jnp.int32, (tq, tq), 0),
                                 lax.broadcasted_iota(jnp.int32, (tq, tq), 1))
    for qi in range(q_ref.shape[0] // tq):
        q0, q1 = qi * tq, (qi + 1) * tq
        heads = []
        for c in range(2):
            cols = slice(c * hd, (c + 1) * hd)
            qc = q_ref[q0:q1, cols]
            scores = [lax.dot_general(qc, k_ref[q0:q1, cols], _NT,
                                      preferred_element_type=F32) + diag_bias]
            values = [v_ref[q0:q1, :]]
            if qi:
                scores.append(lax.dot_general(qc, k_ref[0:q0, cols], _NT,
                                              preferred_element_type=F32))
                values.append(v_ref[0:q0, :])
            heads.append(_softmax_av(scores, values))
        o_ref[q0:q1, :] = _diff_combine(heads, lam_ref, sub_ref, lam_init).astype(o_ref.dtype)


def _attn_prompt(q, k, v, lam_all, sub_all, layer, lam_init, nb, seq, nh):
    m = q.shape[0]
    w = q.shape[1] // nh
    dv = v.shape[1] // nh
    tq = _tile(seq, 256)
    assert tq % CHUNK == 0
    bmap = lambda b, h: (b, h)
    return pl.pallas_call(
        functools.partial(_attn_prompt_kernel, lam_init=lam_init, tq=tq),
        grid=(nb, nh),
        in_specs=[pl.BlockSpec((None, 4, w // 2), lambda b, h: (layer, 0, 0)),
                  pl.BlockSpec((None, 1, dv), lambda b, h: (layer, 0, 0)),
                  pl.BlockSpec((seq, w), bmap),
                  pl.BlockSpec((seq, w), bmap),
                  pl.BlockSpec((seq, dv), bmap)],
        out_specs=pl.BlockSpec((seq, dv), bmap),
        out_shape=jax.ShapeDtypeStruct((m, nh * dv), BF16),
        compiler_params=_params("parallel", "parallel"),
        name="diff_attn_prompt",
    )(lam_all, sub_all, q, k, v)


def _attn_sample_kernel(lam_ref, sub_ref, q_ref, kc_ref, vc_ref, kn_ref, vn_ref, o_ref,
                        *, lam_init):
    tq = q_ref.shape[0]
    past = kc_ref.shape[0]
    tn = kn_ref.shape[0]
    hd = q_ref.shape[1] // 2
    qpos = past + lax.broadcasted_iota(jnp.int32, (tq, 1), 0)
    bias_c = _visibility_bias(qpos, lax.broadcasted_iota(jnp.int32, (tq, past), 1))
    bias_n = _visibility_bias(qpos, past + lax.broadcasted_iota(jnp.int32, (tq, tn), 1))
    values = [vc_ref[...].astype(BF16), vn_ref[...]]
    heads = []
    for c in range(2):
        cols = slice(c * hd, (c + 1) * hd)
        qc = q_ref[:, cols]
        scores = [lax.dot_general(qc, kc_ref[:, cols].astype(BF16), _NT,
                                  preferred_element_type=F32) + bias_c,
                  lax.dot_general(qc, kn_ref[:, cols], _NT,
                                  preferred_element_type=F32) + bias_n]
        heads.append(_softmax_av(scores, values))
    o_ref[...] = _diff_combine(heads, lam_ref, sub_ref, lam_init).astype(o_ref.dtype)


def _attn_sample(q, kn, vn, kc_all, vc_all, lam_all, sub_all, layer, lam_init, nb, seq, nh):
    m = q.shape[0]
    w = q.shape[1] // nh
    dv = vn.shape[1] // nh
    past = kc_all.shape[1]
    cmap = lambda b, h: (layer * nb + b, 0, h)
    nmap = lambda b, h: (b, h)
    return pl.pallas_call(
        functools.partial(_attn_sample_kernel, lam_init=lam_init),
        grid=(nb, nh),
        in_specs=[pl.BlockSpec((None, 4, w // 2), lambda b, h: (layer, 0, 0)),
                  pl.BlockSpec((None, 1, dv), lambda b, h: (layer, 0, 0)),
                  pl.BlockSpec((seq, w), nmap),
                  pl.BlockSpec((None, past, w), cmap),
                  pl.BlockSpec((None, past, dv), cmap),
                  pl.BlockSpec((seq, w), nmap),
                  pl.BlockSpec((seq, dv), nmap)],
        out_specs=pl.BlockSpec((seq, dv), nmap),
        out_shape=jax.ShapeDtypeStruct((m, nh * dv), BF16),
        compiler_params=_params("parallel", "parallel"),
        name="diff_attn_sample",
    )(lam_all, sub_all, q, kc_all, vc_all, kn, vn)


def _rope_tables(pos, hd):
    rot = hd // 4
    half = rot // 2
    inv = ROPE_THETA ** (-jnp.arange(0, rot, 2, dtype=F32) / rot)
    ang = pos.astype(F32)[:, None] * inv[None, :]
    cos, sin = jnp.cos(ang), jnp.sin(ang)
    n = pos.shape[0]
    ones = jnp.ones((n, hd - rot), F32)
    zeros_h = jnp.zeros((n, half), F32)
    zeros_r = jnp.zeros((n, hd - rot), F32)
    tab_cos = jnp.concatenate([cos, cos, ones], axis=1)
    tab_sa = jnp.concatenate([zeros_h, sin, zeros_r], axis=1)
    tab_sb = jnp.concatenate([-sin, zeros_h, zeros_r], axis=1)
    return tab_cos, tab_sa, tab_sb


def kernel(x_prompt, x_sample, cache_k, cache_v, state_gla, norm_mix, w_in, w_gate, b_gate,
           gla_gain, diff_lambda, diff_subln, w_out, norm_mlp, w_up, w_down, norm_final):
    depth, d_model, _ = w_in.shape
    bp, sp, _ = x_prompt.shape
    bs, ss, _ = x_sample.shape
    _, _, past, h_d, _, hd_d = cache_k.shape
    _, _, h_g, dk_g, dv_g = state_gla.shape
    rank = w_gate.shape[1]
    dv_d = cache_v.shape[-1]
    w_gqk = 2 * h_g * dk_g
    w_gvg = 2 * h_g * dv_g
    w_dq = h_d * 2 * hd_d
    w_dv = h_d * dv_d
    assert hd_d == LANES and hd_d // 8 == 16 and rank <= LANES

    o_ga = w_gqk + w_gvg
    w_gla = w_in[:, :, :o_ga].astype(BF16)
    w_diff = w_in[:, :, o_ga + rank:].astype(BF16)
    w_ga = jnp.pad(w_in[:, :, o_ga:o_ga + rank], ((0, 0), (0, 0), (0, LANES - rank))).astype(BF16)
    w_gate_p = jnp.pad(w_gate, ((0, 0), (0, LANES - rank), (0, 0)))
    w_out_b = w_out.astype(BF16)
    w_up_b = w_up.astype(BF16)
    w_down_b = w_down.astype(BF16)
    norm_mix3 = norm_mix[:, None, :]
    norm_mlp3 = norm_mlp[:, None, :]
    b_gate3 = b_gate[:, None, :]
    gain3 = gla_gain[:, None, :]
    subln3 = diff_subln[:, None, :]
    kc_all = cache_k.reshape(depth * bs, past, w_dq)
    vc_all = cache_v.reshape(depth * bs, past, w_dv)
    zero_state = jnp.zeros((1, bp, h_g, dk_g, dv_g), F32)
    tabs_p = _rope_tables(jnp.arange(sp), hd_d)
    tabs_s = _rope_tables(past + jnp.arange(ss), hd_d)

    def layer_fn(stream, l, nb, seq, tabs, s0_all, s0_layer, cached):
        x, xg, ssq = stream
        lam_init = 0.8 - 0.6 * math.exp(-0.3 * l)
        qk, ga = _proj(xg, ssq, w_gla, l, 0, w_gqk, [F32], side_w=w_ga, name="proj_gqk")
        (vg,) = _proj(xg, ssq, w_gla, l, w_gqk, w_gvg, [BF16], name="proj_gvg")
        (qd,) = _proj(xg, ssq, w_diff, l, 0, w_dq, [BF16], rope_tabs=tabs, rows_per_seq=seq,
                      scale=hd_d ** -0.5 * math.log2(math.e), name="proj_dq")
        kd32, kd16 = _proj(xg, ssq, w_diff, l, w_dq, w_dq, [F32, BF16], rope_tabs=tabs,
                           rows_per_seq=seq, split=hd_d, tm_pref=512, name="proj_dk")
        vd32, vd16 = _proj(xg, ssq, w_diff, l, 2 * w_dq, w_dv, [F32, BF16], split=dv_d,
                           tm_pref=256, name="proj_dv")
        yg, s_new = _gla(qk, vg, ga, w_gate_p, b_gate3, gain3, s0_all, l, s0_layer, nb, seq, h_g)
        if cached:
            yd = _attn_sample(qd, kd16, vd16, kc_all, vc_all, diff_lambda, subln3, l, lam_init,
                              nb, seq, h_d)
        else:
            yd = _attn_prompt(qd, kd16, vd16, diff_lambda, subln3, l, lam_init, nb, seq, h_d)
        x, hg, hssq = _outproj(yg, yd, w_out_b, l, x, norm_mlp3)
        (h,) = _proj(hg, hssq, w_up_b, l, 0, w_up_b.shape[2], [BF16], act=True, name="mlp_up")
        stream = _down(h, w_down_b, l, x, norm_mix3, min(l + 1, depth - 1))
        return (stream, kd32.reshape(nb, seq, h_d, 2, hd_d), vd32.reshape(nb, seq, h_d, dv_d),
                s_new)

    def start(x):
        return (x,) + tuple(_prenorm(x, norm_mix3, 0))

    st_p = start(x_prompt.reshape(bp * sp, d_model))
    st_s = start(x_sample.reshape(bs * ss, d_model))
    kp, vp, stp, kss, vss, sts = [], [], [], [], [], []
    for l in range(depth):
        st_p, k_new, v_new, s_new = layer_fn(st_p, l, bp, sp, tabs_p, zero_state, 0, False)
        kp.append(k_new); vp.append(v_new); stp.append(s_new)
        st_s, k_new, v_new, s_new = layer_fn(st_s, l, bs, ss, tabs_s, state_gla, l, True)
        kss.append(k_new); vss.append(v_new); sts.append(s_new)
    nf3 = norm_final[None, None, :]
    y_p = _rmsnorm(st_p[0], nf3, 0, F32).reshape(bp, sp, d_model)
    y_s = _rmsnorm(st_s[0], nf3, 0, F32).reshape(bs, ss, d_model)
    return (y_p, y_s, jnp.stack(kp), jnp.stack(vp), jnp.stack(stp),
            jnp.stack(kss), jnp.stack(vss), jnp.stack(sts))
```

```python
import functools
import math

import jax
import jax.numpy as jnp
from jax import lax
from jax.experimental import pallas as pl
from jax.experimental.pallas import tpu as pltpu

CHUNK = 64
ROPE_THETA = 500000.0
GATE_TAU = 16.0
EPS = 1e-5
LANES = 128
SUBLANES = 8
VMEM_LIMIT = 56 * 1024 * 1024

F32 = jnp.float32
BF16 = jnp.bfloat16
_NT = (((1,), (1,)), ((), ()))
_TN = (((0,), (0,)), ((), ()))


def _tile(dim, pref):
    if dim <= pref:
        return dim
    t = pref
    while dim % t:
        t //= 2
    return t


def _params(*sem):
    return pltpu.CompilerParams(dimension_semantics=sem, vmem_limit_bytes=VMEM_LIMIT)


def _rmsnorm_kernel(x_ref, g_ref, o_ref):
    x = x_ref[...]
    r = lax.rsqrt(jnp.mean(x * x, axis=-1, keepdims=True) + EPS)
    o_ref[...] = (x * r * g_ref[...]).astype(o_ref.dtype)


def _rmsnorm(x, g_all, layer, out_dtype):
    m, d = x.shape
    tm = _tile(m, 512)
    return pl.pallas_call(
        _rmsnorm_kernel,
        grid=(m // tm,),
        in_specs=[pl.BlockSpec((tm, d), lambda i: (i, 0)),
                  pl.BlockSpec((None, 1, d), lambda i: (layer, 0, 0))],
        out_specs=pl.BlockSpec((tm, d), lambda i: (i, 0)),
        out_shape=jax.ShapeDtypeStruct((m, d), out_dtype),
        compiler_params=_params("parallel"),
        name="rmsnorm",
    )(x, g_all)


def _emit_norm_operands(x_new, g_ref, xg_ref, ssq_ref, first):
    xg_ref[...] = (x_new * g_ref[...]).astype(xg_ref.dtype)
    part = jnp.broadcast_to(jnp.sum(x_new * x_new, axis=-1, keepdims=True), ssq_ref.shape)

    @pl.when(first)
    def _():
        ssq_ref[...] = part

    @pl.when(jnp.logical_not(first))
    def _():
        ssq_ref[...] += part


def _prenorm_kernel(x_ref, g_ref, xg_ref, ssq_ref):
    _emit_norm_operands(x_ref[...], g_ref, xg_ref, ssq_ref, pl.program_id(1) == 0)


def _prenorm(x, g_all, layer):
    m, d = x.shape
    tm = _tile(m, 1024)
    tn = _tile(d, 1024)
    return pl.pallas_call(
        _prenorm_kernel,
        grid=(m // tm, d // tn),
        in_specs=[pl.BlockSpec((tm, tn), lambda i, j: (i, j)),
                  pl.BlockSpec((None, 1, tn), lambda i, j: (layer, 0, j))],
        out_specs=[pl.BlockSpec((tm, tn), lambda i, j: (i, j)),
                   pl.BlockSpec((tm, LANES), lambda i, j: (i, 0))],
        out_shape=[jax.ShapeDtypeStruct((m, d), BF16),
                   jax.ShapeDtypeStruct((m, LANES), F32)],
        compiler_params=_params("parallel", "arbitrary"),
        name="prenorm",
    )(x, g_all)


def _proj_kernel(*refs, rope, scale, act, split, side, n_out):
    a_ref, w_ref, ssq_ref = refs[0], refs[1], refs[2]
    n_in = 4 if side else 3
    outs = refs[len(refs) - n_out:]
    z = jnp.dot(a_ref[...], w_ref[...], preferred_element_type=F32)
    tn = z.shape[1]
    reps = tn // LANES
    r = lax.rsqrt(ssq_ref[...] * (1.0 / a_ref.shape[1]) + EPS)
    z = z * jnp.tile(r, (1, reps))
    if side:
        side_ref, outs = outs[-1], outs[:-1]

        @pl.when(pl.program_id(1) == 0)
        def _():
            zs = jnp.dot(a_ref[...], refs[3][...], preferred_element_type=F32)
            side_ref[...] = (zs * r).astype(side_ref.dtype)

    if rope:
        cos_ref, sa_ref, sb_ref = refs[n_in:n_in + 3]
        cos = jnp.tile(cos_ref[...], (1, reps))
        sa = jnp.tile(sa_ref[...], (1, reps))
        sb = jnp.tile(sb_ref[...], (1, reps))
        z = z * cos + pltpu.roll(z, 16, 1) * sa + pltpu.roll(z, tn - 16, 1) * sb
    if scale != 1.0:
        z = z * scale
    if act:
        z = jnp.square(jnp.maximum(z, 0.0))
    if split:
        o3, outs = outs[0], outs[1:]
        for g in range(o3.shape[1]):
            o3[:, g, :] = z[:, g * split:(g + 1) * split].astype(o3.dtype)
    for o in outs:
        o[...] = z.astype(o.dtype)


def _proj(a, ssq, w_all, layer, col0, ncols, out_dtypes, *, rope_tabs=None, rows_per_seq=None,
          scale=1.0, act=False, split=0, side_w=None, stack=None, tm_pref=1024, name="proj"):
    m, k = a.shape
    tm = _tile(m, tm_pref)
    if split:
        tn = SUBLANES * split
        assert col0 % tn == 0 and ncols % tn == 0
    else:
        tn = _tile(math.gcd(col0, ncols), 512 if rope_tabs is not None else 1024)
    jb = col0 // tn
    in_specs = [pl.BlockSpec((tm, k), lambda i, j: (i, 0)),
                pl.BlockSpec((None, k, tn), lambda i, j: (layer, 0, j + jb)),
                pl.BlockSpec((tm, LANES), lambda i, j: (i, 0))]
    args = [a, w_all, ssq]
    if side_w is not None:
        in_specs.append(pl.BlockSpec((None, k, LANES), lambda i, j: (layer, 0, 0)))
        args.append(side_w)
    if rope_tabs is not None:
        assert rows_per_seq % tm == 0 or tm % rows_per_seq == 0
        if tm >= rows_per_seq:
            tabs = [jnp.tile(t, (tm // rows_per_seq, 1)) for t in rope_tabs]
            tmap = lambda i, j: (0, 0)
        else:
            tabs = list(rope_tabs)
            nblk = rows_per_seq // tm
            tmap = lambda i, j: (i % nblk, 0)
        in_specs += [pl.BlockSpec((tm, LANES), tmap)] * 3
        args += tabs
    n_out = len(out_dtypes)
    out_specs = [pl.BlockSpec((tm, tn), lambda i, j: (i, j))] * n_out
    out_shape = [jax.ShapeDtypeStruct((m, ncols), dt) for dt in out_dtypes]
    if split:
        out_specs[0] = pl.BlockSpec((tm, SUBLANES, split), lambda i, j: (i, j, 0))
        out_shape[0] = jax.ShapeDtypeStruct((m, ncols // split, split), out_dtypes[0])
    if side_w is not None:
        out_specs.append(pl.BlockSpec((tm, LANES), lambda i, j: (i, 0)))
        out_shape.append(jax.ShapeDtypeStruct((m, LANES), F32))
    aliases = {}
    if stack is not None:
        buf, slab = stack
        assert split and buf.shape == (buf.shape[0] // m * m, ncols // split, split)
        row0 = slab * (m // tm)
        out_specs[0] = pl.BlockSpec((tm, SUBLANES, split), lambda i, j: (row0 + i, j, 0))
        out_shape[0] = jax.ShapeDtypeStruct(buf.shape, buf.dtype)
        aliases = {len(args): 0}
        in_specs.append(pl.BlockSpec(memory_space=pl.ANY))
        args.append(buf)
    outs = pl.pallas_call(
        functools.partial(_proj_kernel, rope=rope_tabs is not None, scale=scale, act=act,
                          split=split, side=side_w is not None, n_out=len(out_specs)),
        grid=(m // tm, ncols // tn),
        in_specs=in_specs,
        out_specs=out_specs,
        out_shape=out_shape,
        input_output_aliases=aliases,
        compiler_params=_params("parallel", "arbitrary" if side_w is not None else "parallel"),
        name=name,
    )(*args)
    return outs


def _outproj_kernel(yg_ref, yd_ref, w_ref, res_ref, g_ref, o_ref, xg_ref, ssq_ref):
    kg = yg_ref.shape[1]
    z = jnp.dot(yg_ref[...], w_ref[:kg, :], preferred_element_type=F32)
    z = z + jnp.dot(yd_ref[...], w_ref[kg:, :], preferred_element_type=F32)
    x_new = res_ref[...] + z
    o_ref[...] = x_new
    _emit_norm_operands(x_new, g_ref, xg_ref, ssq_ref, pl.program_id(1) == 0)


def _outproj(yg, yd, w_all, layer, res, g_all):
    m, kg = yg.shape
    kd = yd.shape[1]
    n = w_all.shape[2]
    tm = _tile(m, 1024)
    tn = _tile(n, 512)
    return pl.pallas_call(
        _outproj_kernel,
        grid=(m // tm, n // tn),
        in_specs=[pl.BlockSpec((tm, kg), lambda i, j: (i, 0)),
                  pl.BlockSpec((tm, kd), lambda i, j: (i, 0)),
                  pl.BlockSpec((None, kg + kd, tn), lambda i, j: (layer, 0, j)),
                  pl.BlockSpec((tm, tn), lambda i, j: (i, j)),
                  pl.BlockSpec((None, 1, tn), lambda i, j: (layer, 0, j))],
        out_specs=[pl.BlockSpec((tm, tn), lambda i, j: (i, j)),
                   pl.BlockSpec((tm, tn), lambda i, j: (i, j)),
                   pl.BlockSpec((tm, LANES), lambda i, j: (i, 0))],
        out_shape=[jax.ShapeDtypeStruct((m, n), F32),
                   jax.ShapeDtypeStruct((m, n), BF16),
                   jax.ShapeDtypeStruct((m, LANES), F32)],
        compiler_params=_params("parallel", "arbitrary"),
        name="outproj",
    )(yg, yd, w_all, res, g_all)


def _down_kernel(h_ref, w_ref, res_ref, g_ref, o_ref, xg_ref, ssq_ref):
    z = jnp.dot(h_ref[...], w_ref[...], preferred_element_type=F32)
    kk = pl.program_id(2)

    @pl.when(kk == 0)
    def _():
        o_ref[...] = res_ref[...] + z

    @pl.when(kk > 0)
    def _():
        o_ref[...] += z

    @pl.when(kk == pl.num_programs(2) - 1)
    def _():
        _emit_norm_operands(o_ref[...], g_ref, xg_ref, ssq_ref, pl.program_id(1) == 0)


def _down(h, w_all, layer, res, g_all, g_layer):
    m, k = h.shape
    n = w_all.shape[2]
    tm = _tile(m, 1024)
    tn = _tile(n, 1024)
    tk = _tile(k, 2048)
    return pl.pallas_call(
        _down_kernel,
        grid=(m // tm, n // tn, k // tk),
        in_specs=[pl.BlockSpec((tm, tk), lambda i, j, kk: (i, kk)),
                  pl.BlockSpec((None, tk, tn), lambda i, j, kk: (layer, kk, j)),
                  pl.BlockSpec((tm, tn), lambda i, j, kk: (i, j)),
                  pl.BlockSpec((None, 1, tn), lambda i, j, kk: (g_layer, 0, j))],
        out_specs=[pl.BlockSpec((tm, tn), lambda i, j, kk: (i, j)),
                   pl.BlockSpec((tm, tn), lambda i, j, kk: (i, j)),
                   pl.BlockSpec((tm, LANES), lambda i, j, kk: (i, 0))],
        out_shape=[jax.ShapeDtypeStruct((m, n), F32),
                   jax.ShapeDtypeStruct((m, n), BF16),
                   jax.ShapeDtypeStruct((m, LANES), F32)],
        compiler_params=_params("parallel", "arbitrary", "arbitrary"),
        name="mlp_down",
    )(h, w_all, res, g_all)


def _split2(x):
    hi = x.astype(BF16)
    lo = (x - hi.astype(F32)).astype(BF16)
    return hi, lo


def _gla_kernel(q_ref, k_ref, v_ref, gg_ref, ga_ref, wg_ref, bg_ref, gain_ref, s0_ref,
                y_ref, sout_ref, st_ref, la_ref, qe_ref, kd_ref, oi_ref, ebl_ref,
                *, chunk, q_scale):
    t = pl.program_id(2)
    nchunk = q_ref.shape[0] // chunk
    unroll = 2 if nchunk % 2 == 0 else 1

    @pl.when(t == 0)
    def _():
        st_ref[...] = s0_ref[...].T

    ga_hi, ga_lo = _split2(ga_ref[...])
    w_hi, w_lo = _split2(wg_ref[...])
    pre = (jnp.dot(ga_hi, w_hi, preferred_element_type=F32)
           + jnp.dot(ga_lo, w_hi, preferred_element_type=F32)
           + jnp.dot(ga_hi, w_lo, preferred_element_type=F32)) + bg_ref[...]
    la_ref[...] = jax.nn.log_sigmoid(pre) / GATE_TAU

    row = lax.broadcasted_iota(jnp.int32, (chunk, chunk), 0)
    col = lax.broadcasted_iota(jnp.int32, (chunk, chunk), 1)
    tri = row >= col
    tri_b = tri.astype(BF16)

    group = 8 if nchunk % 8 == 0 else 1
    dk = q_ref.shape[1]

    def local(gi, carry):
        rows = pl.ds(pl.multiple_of(gi * (group * chunk), group * chunk), group * chunk)
        la_hi, la_lo = _split2(la_ref[rows, :])
        cs = [slice(g * chunk, (g + 1) * chunk) for g in range(group)]
        b_parts = [jnp.dot(tri_b, la_hi[c], preferred_element_type=F32)
                   + jnp.dot(tri_b, la_lo[c], preferred_element_type=F32)
                   for c in cs]
        bl_rows = [bp[chunk - 1:chunk, :] for bp in b_parts]
        b = jnp.concatenate(b_parts, axis=0)
        bl = jnp.concatenate([jnp.broadcast_to(r, (chunk, dk)) for r in bl_rows], axis=0)
        k = k_ref[rows, :]
        v = v_ref[rows, :]
        qe = (q_ref[rows, :] * q_scale * jnp.exp(b)).astype(BF16)
        ke = (k * jnp.exp(-b)).astype(BF16)
        oi_parts = []
        for c in cs:
            a = lax.dot_general(qe[c], ke[c], _NT, preferred_element_type=F32)
            a = jnp.where(tri, a, 0.0)
            oi_parts.append(jnp.dot(a.astype(BF16), v[c], preferred_element_type=F32))
        oi_ref[rows, :] = jnp.concatenate(oi_parts, axis=0)
        qe_ref[rows, :] = qe
        kd_ref[rows, :] = (k * jnp.exp(bl - b)).astype(BF16)
        ebl_ref[pl.ds(pl.multiple_of(gi * group, group), group), :] = jnp.exp(
            jnp.concatenate(bl_rows, axis=0))
        return carry

    lax.fori_loop(0, nchunk // group, local, 0)

    def serial(c, carry):
        rows = pl.ds(pl.multiple_of(c * chunk, chunk), chunk)
        st = st_ref[...]
        o = lax.dot_general(qe_ref[rows, :], st.astype(BF16), _NT, preferred_element_type=F32)
        o = o + oi_ref[rows, :]
        st_ref[...] = st * ebl_ref[pl.ds(c, 1), :] + lax.dot_general(
            v_ref[rows, :], kd_ref[rows, :], _TN, preferred_element_type=F32)
        r = lax.rsqrt(jnp.mean(o * o, axis=-1, keepdims=True) + EPS)
        gg = gg_ref[rows, :].astype(F32)
        y = (o * r * gain_ref[...]) * (gg * jax.nn.sigmoid(gg))
        y_ref[rows, :] = y.astype(y_ref.dtype)
        return carry

    lax.fori_loop(0, nchunk, serial, 0, unroll=unroll)

    @pl.when(t == pl.num_programs(2) - 1)
    def _():
        sout_ref[...] = st_ref[...].T


def _gla(qk, vg, ga, wg_all, bg_all, gain_all, s0_all, layer, s0_layer, nb, seq, nh):
    m = qk.shape[0]
    dk = qk.shape[1] // (2 * nh)
    dv = vg.shape[1] // (2 * nh)
    chunk = min(seq, CHUNK)
    tt = _tile(seq, 1024)
    nt = seq // tt
    rmap = lambda b, h, t: (b * nt + t, h)
    rmap2 = lambda b, h, t: (b * nt + t, h + nh)
    y, s_out = pl.pallas_call(
        functools.partial(_gla_kernel, chunk=chunk, q_scale=dk ** -0.5),
        grid=(nb, nh, nt),
        in_specs=[pl.BlockSpec((tt, dk), rmap),
                  pl.BlockSpec((tt, dk), rmap2),
                  pl.BlockSpec((tt, dv), rmap),
                  pl.BlockSpec((tt, dv), rmap2),
                  pl.BlockSpec((tt, LANES), lambda b, h, t: (b * nt + t, 0)),
                  pl.BlockSpec((None, LANES, dk), lambda b, h, t: (layer, 0, h)),
                  pl.BlockSpec((None, 1, dk), lambda b, h, t: (layer, 0, h)),
                  pl.BlockSpec((None, 1, dv), lambda b, h, t: (layer, 0, 0)),
                  pl.BlockSpec((None, None, None, dk, dv),
                               lambda b, h, t: (s0_layer, b, h, 0, 0))],
        out_specs=[pl.BlockSpec((tt, dv), rmap),
                   pl.BlockSpec((None, None, dk, dv), lambda b, h, t: (b, h, 0, 0))],
        out_shape=[jax.ShapeDtypeStruct((m, nh * dv), BF16),
                   jax.ShapeDtypeStruct((nb, nh, dk, dv), F32)],
        scratch_shapes=[pltpu.VMEM((dv, dk), F32),
                        pltpu.VMEM((tt, dk), F32),
                        pltpu.VMEM((tt, dk), BF16),
                        pltpu.VMEM((tt, dk), BF16),
                        pltpu.VMEM((tt, dv), F32),
                        pltpu.VMEM((max(tt // chunk, 8), dk), F32)],
        compiler_params=_params("parallel", "parallel", "arbitrary"),
        name="gla",
    )(qk, qk, vg, vg, ga, wg_all, bg_all, gain_all, s0_all)
    return y, s_out


def _softmax_av(score_parts, value_parts):
    m = functools.reduce(jnp.maximum,
                         [jnp.max(s, axis=-1, keepdims=True) for s in score_parts])
    ps = [jnp.exp2(s - m) for s in score_parts]
    l = functools.reduce(jnp.add, [jnp.sum(p, axis=-1, keepdims=True) for p in ps])
    acc = functools.reduce(jnp.add, [jnp.dot(p.astype(BF16), v, preferred_element_type=F32)
                                     for p, v in zip(ps, value_parts)])
    return acc * (1.0 / l)


def _diff_combine(heads, lam_ref, sub_ref, lam_init):
    lp = lam_ref[...]
    lam = (jnp.exp(jnp.sum(lp[0:1] * lp[1:2], axis=-1, keepdims=True))
           - jnp.exp(jnp.sum(lp[2:3] * lp[3:4], axis=-1, keepdims=True)) + lam_init)
    o = heads[0] - lam * heads[1]
    r = lax.rsqrt(jnp.mean(o * o, axis=-1, keepdims=True) + EPS)
    return (o * r * sub_ref[...]) * (1.0 - lam_init)


def _visibility_bias(qpos, kpos):
    return jnp.where((kpos // CHUNK) <= (qpos // CHUNK), 0.0, -jnp.inf).astype(F32)


def _attn_prompt_kernel(lam_ref, sub_ref, q_ref, k_ref, v_ref, o_ref, *, lam_init, tq):
    hd = q_ref.shape[1] // 2
    diag_bias = _visibility_bias(lax.broadcasted_iota(jnp.int32, (tq, tq), 0),
                                 lax.broadcasted_iota(jnp.int32, (tq, tq), 1))
    for qi in range(q_ref.shape[0] // tq):
        q0, q1 = qi * tq, (qi + 1) * tq
        heads = []
        for c in range(2):
            cols = slice(c * hd, (c + 1) * hd)
            qc = q_ref[q0:q1, cols]
            scores = [lax.dot_general(qc, k_ref[q0:q1, cols], _NT,
                                      preferred_element_type=F32) + diag_bias]
            values = [v_ref[q0:q1, :]]
            if qi:
                scores.append(lax.dot_general(qc, k_ref[0:q0, cols], _NT,
                                              preferred_element_type=F32))
                values.append(v_ref[0:q0, :])
            heads.append(_softmax_av(scores, values))
        o_ref[q0:q1, :] = _diff_combine(heads, lam_ref, sub_ref, lam_init).astype(o_ref.dtype)


def _attn_prompt(q, k, v, lam_all, sub_all, layer, lam_init, nb, seq, nh):
    m = q.shape[0]
    w = q.shape[1] // nh
    dv = v.shape[1] // nh
    tq = _tile(seq, 256)
    assert tq % CHUNK == 0
    bmap = lambda b, h: (b, h)
    return pl.pallas_call(
        functools.partial(_attn_prompt_kernel, lam_init=lam_init, tq=tq),
        grid=(nb, nh),
        in_specs=[pl.BlockSpec((None, 4, w // 2), lambda b, h: (layer, 0, 0)),
                  pl.BlockSpec((None, 1, dv), lambda b, h: (layer, 0, 0)),
                  pl.BlockSpec((seq, w), bmap),
                  pl.BlockSpec((seq, w), bmap),
                  pl.BlockSpec((seq, dv), bmap)],
        out_specs=pl.BlockSpec((seq, dv), bmap),
        out_shape=jax.ShapeDtypeStruct((m, nh * dv), BF16),
        compiler_params=_params("parallel", "parallel"),
        name="diff_attn_prompt",
    )(lam_all, sub_all, q, k, v)


def _attn_sample_kernel(lam_ref, sub_ref, q_ref, kc_ref, vc_ref, kn_ref, vn_ref, o_ref,
                        *, lam_init):
    tq = q_ref.shape[0]
    past = kc_ref.shape[0]
    tn = kn_ref.shape[0]
    hd = q_ref.shape[1] // 2
    qpos = past + lax.broadcasted_iota(jnp.int32, (tq, 1), 0)
    bias_c = _visibility_bias(qpos, lax.broadcasted_iota(jnp.int32, (tq, past), 1))
    bias_n = _visibility_bias(qpos, past + lax.broadcasted_iota(jnp.int32, (tq, tn), 1))
    values = [vc_ref[...].astype(BF16), vn_ref[...]]
    heads = []
    for c in range(2):
        cols = slice(c * hd, (c + 1) * hd)
        qc = q_ref[:, cols]
        scores = [lax.dot_general(qc, kc_ref[:, cols].astype(BF16), _NT,
                                  preferred_element_type=F32) + bias_c,
                  lax.dot_general(qc, kn_ref[:, cols], _NT,
                                  preferred_element_type=F32) + bias_n]
        heads.append(_softmax_av(scores, values))
    o_ref[...] = _diff_combine(heads, lam_ref, sub_ref, lam_init).astype(o_ref.dtype)


def _attn_sample(q, kn, vn, kc_all, vc_all, lam_all, sub_all, layer, lam_init, nb, seq, nh):
    m = q.shape[0]
    w = q.shape[1] // nh
    dv = vn.shape[1] // nh
    past = kc_all.shape[1]
    cmap = lambda b, h: (layer * nb + b, 0, h)
    nmap = lambda b, h: (b, h)
    return pl.pallas_call(
        functools.partial(_attn_sample_kernel, lam_init=lam_init),
        grid=(nb, nh),
        in_specs=[pl.BlockSpec((None, 4, w // 2), lambda b, h: (layer, 0, 0)),
                  pl.BlockSpec((None, 1, dv), lambda b, h: (layer, 0, 0)),
                  pl.BlockSpec((seq, w), nmap),
                  pl.BlockSpec((None, past, w), cmap),
                  pl.BlockSpec((None, past, dv), cmap),
                  pl.BlockSpec((seq, w), nmap),
                  pl.BlockSpec((seq, dv), nmap)],
        out_specs=pl.BlockSpec((seq, dv), nmap),
        out_shape=jax.ShapeDtypeStruct((m, nh * dv), BF16),
        compiler_params=_params("parallel", "parallel"),
        name="diff_attn_sample",
    )(lam_all, sub_all, q, kc_all, vc_all, kn, vn)


def _rope_tables(pos, hd):
    rot = hd // 4
    half = rot // 2
    inv = ROPE_THETA ** (-jnp.arange(0, rot, 2, dtype=F32) / rot)
    ang = pos.astype(F32)[:, None] * inv[None, :]
    cos, sin = jnp.cos(ang), jnp.sin(ang)
    n = pos.shape[0]
    ones = jnp.ones((n, hd - rot), F32)
    zeros_h = jnp.zeros((n, half), F32)
    zeros_r = jnp.zeros((n, hd - rot), F32)
    tab_cos = jnp.concatenate([cos, cos, ones], axis=1)
    tab_sa = jnp.concatenate([zeros_h, sin, zeros_r], axis=1)
    tab_sb = jnp.concatenate([-sin, zeros_h, zeros_r], axis=1)
    return tab_cos, tab_sa, tab_sb


def kernel(x_prompt, x_sample, cache_k, cache_v, state_gla, norm_mix, w_in, w_gate, b_gate,
           gla_gain, diff_lambda, diff_subln, w_out, norm_mlp, w_up, w_down, norm_final):
    depth, d_model, _ = w_in.shape
    bp, sp, _ = x_prompt.shape
    bs, ss, _ = x_sample.shape
    _, _, past, h_d, _, hd_d = cache_k.shape
    _, _, h_g, dk_g, dv_g = state_gla.shape
    rank = w_gate.shape[1]
    dv_d = cache_v.shape[-1]
    w_gqk = 2 * h_g * dk_g
    w_gvg = 2 * h_g * dv_g
    w_dq = h_d * 2 * hd_d
    w_dv = h_d * dv_d
    assert hd_d == LANES and hd_d // 8 == 16 and rank <= LANES

    o_ga = w_gqk + w_gvg
    w_gla = w_in[:, :, :o_ga].astype(BF16)
    w_diff = w_in[:, :, o_ga + rank:].astype(BF16)
    w_ga = jnp.pad(w_in[:, :, o_ga:o_ga + rank], ((0, 0), (0, 0), (0, LANES - rank))).astype(BF16)
    w_gate_p = jnp.pad(w_gate, ((0, 0), (0, LANES - rank), (0, 0)))
    w_out_b = w_out.astype(BF16)
    w_up_b = w_up.astype(BF16)
    w_down_b = w_down.astype(BF16)
    norm_mix3 = norm_mix[:, None, :]
    norm_mlp3 = norm_mlp[:, None, :]
    b_gate3 = b_gate[:, None, :]
    gain3 = gla_gain[:, None, :]
    subln3 = diff_subln[:, None, :]
    kc_all = cache_k.reshape(depth * bs, past, w_dq)
    vc_all = cache_v.reshape(depth * bs, past, w_dv)
    zero_state = jnp.zeros((1, bp, h_g, dk_g, dv_g), F32)
    tabs_p = _rope_tables(jnp.arange(sp), hd_d)
    tabs_s = _rope_tables(past + jnp.arange(ss), hd_d)

    def layer_fn(stream, kbuf, vbuf, l, nb, seq, tabs, s0_all, s0_layer, cached):
        x, xg, ssq = stream
        lam_init = 0.8 - 0.6 * math.exp(-0.3 * l)
        qk, ga = _proj(xg, ssq, w_gla, l, 0, w_gqk, [F32], side_w=w_ga, name="proj_gqk")
        (vg,) = _proj(xg, ssq, w_gla, l, w_gqk, w_gvg, [BF16], name="proj_gvg")
        (qd,) = _proj(xg, ssq, w_diff, l, 0, w_dq, [BF16], rope_tabs=tabs, rows_per_seq=seq,
                      scale=hd_d ** -0.5 * math.log2(math.e), name="proj_dq")
        kbuf, kd16 = _proj(xg, ssq, w_diff, l, w_dq, w_dq, [F32, BF16], rope_tabs=tabs,
                           rows_per_seq=seq, split=hd_d, stack=(kbuf, l), tm_pref=512,
                           name="proj_dk")
        vbuf, vd16 = _proj(xg, ssq, w_diff, l, 2 * w_dq, w_dv, [F32, BF16], split=dv_d,
                           stack=(vbuf, l), tm_pref=256, name="proj_dv")
        yg, s_new = _gla(qk, vg, ga, w_gate_p, b_gate3, gain3, s0_all, l, s0_layer, nb, seq, h_g)
        if cached:
            yd = _attn_sample(qd, kd16, vd16, kc_all, vc_all, diff_lambda, subln3, l, lam_init,
                              nb, seq, h_d)
        else:
            yd = _attn_prompt(qd, kd16, vd16, diff_lambda, subln3, l, lam_init, nb, seq, h_d)
        x, hg, hssq = _outproj(yg, yd, w_out_b, l, x, norm_mlp3)
        (h,) = _proj(hg, hssq, w_up_b, l, 0, w_up_b.shape[2], [BF16], act=True, name="mlp_up")
        stream = _down(h, w_down_b, l, x, norm_mix3, min(l + 1, depth - 1))
        return stream, kbuf, vbuf, s_new

    def start(x, nb, seq):
        m = nb * seq
        return ((x.reshape(m, d_model),) + tuple(_prenorm(x.reshape(m, d_model), norm_mix3, 0)),
                jnp.zeros((depth * m, 2 * h_d, hd_d), F32), jnp.zeros((depth * m, h_d, dv_d), F32))

    st_p, kp, vp = start(x_prompt, bp, sp)
    st_s, kss, vss = start(x_sample, bs, ss)
    stp, sts = [], []
    for l in range(depth):
        st_p, kp, vp, s_new = layer_fn(st_p, kp, vp, l, bp, sp, tabs_p, zero_state, 0, False)
        stp.append(s_new)
        st_s, kss, vss, s_new = layer_fn(st_s, kss, vss, l, bs, ss, tabs_s, state_gla, l, True)
        sts.append(s_new)
    nf3 = norm_final[None, None, :]
    y_p = _rmsnorm(st_p[0], nf3, 0, F32).reshape(bp, sp, d_model)
    y_s = _rmsnorm(st_s[0], nf3, 0, F32).reshape(bs, ss, d_model)
    return (y_p, y_s,
            kp.reshape(depth, bp, sp, h_d, 2, hd_d), vp.reshape(depth, bp, sp, h_d, dv_d),
            jnp.stack(stp),
            kss.reshape(depth, bs, ss, h_d, 2, hd_d), vss.reshape(depth, bs, ss, h_d, dv_d),
            jnp.stack(sts))
```
